```python
import jax, jax.numpy as jnp
from jax import lax
import numpy as np

D_MODEL = 1024
BATCH = 16
SEQ = 4096
DEPTH = 2

N_MIXERS = 2
N_META = 16
BLOCK = 128
N_PAD = BLOCK - N_META
BRANCH = D_MODEL
HG_DK = 128
HG_DV = 128
HG_HEADS = BRANCH // HG_DV
HG_CHUNK = BLOCK
HG_SUB = 16
N_SUB = HG_CHUNK // HG_SUB
SB_DH = 128
SB_HEADS = BRANCH // SB_DH
SB_SCALE = SB_DH ** -0.5
N_HGRN = (DEPTH + 1) // 2
N_SB = DEPTH // 2
EPS = 1e-6

kernel_name = "hybrid_hgrn2_stickbreaking_meta"


def _rmsnorm(x, w):
    xf = x.astype(jnp.float32)
    y = xf * lax.rsqrt(jnp.mean(xf * xf, axis=-1, keepdims=True) + EPS)
    return (y * w.astype(jnp.float32)).astype(x.dtype)


def _heads(t, n):
    b, l, _ = t.shape
    return t.reshape(b, l, n, -1).transpose(0, 2, 1, 3)


def _pad_front(t):
    return jnp.pad(t, ((0, 0), (0, 0), (N_PAD, 0), (0, 0)))


def _to_blocks(t):
    b, h, lp, d = t.shape
    return t.reshape(b, h, lp // BLOCK, BLOCK, d).transpose(2, 0, 1, 3, 4)


def _hgrn2_chunk(S, inp):
    q, k, v, g = inp
    bsz, h = q.shape[:2]
    b = jnp.cumsum(g, axis=2)
    o = jnp.einsum('bhtd,bhde->bhte', q * jnp.exp(b), S)
    qs = q.reshape(bsz, h, N_SUB, HG_SUB, HG_DK)
    ks = k.reshape(bsz, h, N_SUB, HG_SUB, HG_DK)
    vs = v.reshape(bsz, h, N_SUB, HG_SUB, HG_DV)
    bs = b.reshape(bsz, h, N_SUB, HG_SUB, HG_DK)
    tri = jnp.tril(jnp.ones((HG_SUB, HG_SUB), bool))[:, :, None]
    diff = bs[:, :, :, :, None, :] - bs[:, :, :, None, :, :]
    dec = jnp.exp(jnp.where(tri, diff, -jnp.inf))
    a_diag = jnp.einsum('bhntd,bhntsd,bhnsd->bhnts', qs, dec, ks)
    o_diag = jnp.einsum('bhnts,bhnse->bhnte', a_diag, vs)
    ref = jnp.concatenate([jnp.zeros_like(bs[:, :, :1, 0, :]), bs[:, :, :-1, -1, :]], axis=2)
    q_hat = qs * jnp.exp(bs - ref[:, :, :, None, :])
    lower = (jnp.arange(N_SUB)[None, :] < jnp.arange(N_SUB)[:, None])[:, :, None, None]
    expo = ref[:, :, :, None, None, :] - bs[:, :, None, :, :, :]
    k_hat = ks[:, :, None] * jnp.exp(jnp.where(lower, expo, -jnp.inf))
    a_off = jnp.einsum('bhitd,bhijsd->bhitjs', q_hat, k_hat)
    o_off = jnp.einsum('bhitjs,bhjse->bhite', a_off, vs)
    o = o + (o_diag + o_off).reshape(bsz, h, HG_CHUNK, HG_DV)
    b_last = b[:, :, -1, :]
    S = jnp.exp(b_last)[..., None] * S + jnp.einsum('bhsd,bhse->bhde', k * jnp.exp(b_last[:, :, None, :] - b), v)
    return S, o


def _hgrn2_mixer(y, w_in, lb, out_norm, w_out):
    bsz, l, _ = y.shape
    q, fz, v, gate = jnp.split(y @ w_in, 4, axis=-1)
    q, fz, v = (_heads(t, HG_HEADS).astype(jnp.float32) for t in (q, fz, v))
    lb = lb.reshape(1, HG_HEADS, 1, HG_DK)
    g = jnp.log(lb + (1.0 - lb) * jax.nn.sigmoid(fz))
    k = (1.0 - lb) * jax.nn.sigmoid(-fz)
    xs = tuple(_to_blocks(_pad_front(t)) for t in (q, k, v, g))
    s0 = jnp.zeros((bsz, HG_HEADS, HG_DK, HG_DV), jnp.float32)
    _, o = lax.scan(_hgrn2_chunk, s0, xs)
    nb = o.shape[0]
    o = o.transpose(1, 2, 0, 3, 4).reshape(bsz, HG_HEADS, nb * HG_CHUNK, HG_DV)[:, :, N_PAD:]
    o = _rmsnorm(o.transpose(0, 2, 1, 3), out_norm).reshape(bsz, l, BRANCH)
    return (o.astype(y.dtype) * jax.nn.silu(gate)) @ w_out


def _stick_breaking_mixer(y, w_in, w_out):
    bsz, l, _ = y.shape
    q, k, v, gate = jnp.split(y @ w_in, 4, axis=-1)
    q, k, v = (_pad_front(_heads(t, SB_HEADS).astype(jnp.float32)) for t in (q, k, v))
    lp = l + N_PAD
    s_pos = jnp.arange(lp)
    key_ok = s_pos >= N_PAD

    def block(args):
        qb, n = args
        t_pos = n * BLOCK + jnp.arange(BLOCK)
        valid = (s_pos[None, :] < t_pos[:, None]) & key_ok[None, :]
        z = jnp.einsum('bhtd,bhsd->bhts', qb, k) * SB_SCALE
        log_beta = jax.nn.log_sigmoid(z)
        log_keep = jnp.where(valid, log_beta - z, 0.0)
        later = lax.cumsum(log_keep, axis=3, reverse=True) - log_keep
        a = jnp.where(valid, jnp.exp(log_beta + later), 0.0)
        return jnp.einsum('bhts,bhse->bhte', a, v)

    o = lax.map(block, (_to_blocks(q), jnp.arange(lp // BLOCK)))
    o = o.transpose(1, 0, 3, 2, 4).reshape(bsz, lp, BRANCH)[:, N_PAD:]
    return (o.astype(y.dtype) * jax.nn.silu(gate)) @ w_out


def setup_inputs(seed: int = 0) -> dict:
    key = jax.random.key(seed)
    ks = jax.random.split(key, 11)
    f32 = jnp.float32
    x = jax.random.normal(ks[0], (BATCH, SEQ, D_MODEL), f32)
    meta_tokens = jax.random.normal(ks[1], (N_META, D_MODEL), f32)
    pre_norm = 1.0 + 0.05 * jax.random.normal(ks[2], (DEPTH, D_MODEL), f32)
    post_norm = 1.0 + 0.05 * jax.random.normal(ks[3], (DEPTH, D_MODEL), f32)
    hgrn_w_in = jax.random.normal(ks[4], (N_HGRN, D_MODEL, 4 * BRANCH), f32) * D_MODEL ** -0.5
    hgrn_lb = 0.5 * jax.random.normal(ks[5], (N_HGRN + 1, BRANCH), f32)
    hgrn_out_norm = 1.0 + 0.05 * jax.random.normal(ks[6], (N_HGRN, HG_DV), f32)
    hgrn_w_out = jax.random.normal(ks[7], (N_HGRN, BRANCH, D_MODEL), f32) * BRANCH ** -0.5
    sb_w_in = jax.random.normal(ks[8], (N_SB, D_MODEL, 4 * BRANCH), f32) * D_MODEL ** -0.5
    sb_w_out = jax.random.normal(ks[9], (N_SB, BRANCH, D_MODEL), f32) * BRANCH ** -0.5
    return {"x": x, "meta_tokens": meta_tokens, "pre_norm": pre_norm, "post_norm": post_norm,
            "hgrn_w_in": hgrn_w_in, "hgrn_lb": hgrn_lb, "hgrn_out_norm": hgrn_out_norm,
            "hgrn_w_out": hgrn_w_out, "sb_w_in": sb_w_in, "sb_w_out": sb_w_out}


def reference(x, meta_tokens, pre_norm, post_norm, hgrn_w_in, hgrn_lb, hgrn_out_norm, hgrn_w_out, sb_w_in, sb_w_out):
    bsz = x.shape[0]
    meta = jnp.broadcast_to(meta_tokens[None].astype(x.dtype), (bsz, N_META, D_MODEL))
    h = jnp.concatenate([meta, x], axis=1)
    lbs = jnp.cumsum(jax.nn.softmax(hgrn_lb.astype(jnp.float32), axis=0), axis=0)
    for i in range(DEPTH):
        y = _rmsnorm(h, pre_norm[i])
        j = i // N_MIXERS
        if i % N_MIXERS == 0:
            y = _hgrn2_mixer(y, hgrn_w_in[j], lbs[j], hgrn_out_norm[j], hgrn_w_out[j])
        else:
            y = _stick_breaking_mixer(y, sb_w_in[j], sb_w_out[j])
        h = h + _rmsnorm(y, post_norm[i])
    return h[:, N_META:]
```

```python
import functools
import math

import jax
import jax.numpy as jnp
from jax import lax
from jax.experimental import pallas as pl
from jax.experimental.pallas import tpu as pltpu

LANE = 128
SUBLANE = 8
HEAD_DIM = 128
BLOCK = 128
N_META = 16
EPS = 1e-6
LOG2E = 1.4426950408889634
VMEM_LIMIT = 48 * 1024 * 1024
F32 = jnp.float32
BF16 = jnp.bfloat16


def _largest_tile(n, cap, align=16):
    best = None
    for t in range(align, min(n, cap) + 1, align):
        if n % t == 0:
            best = t
    assert best is not None, (n, cap)
    return best


def _in_proj_kernel(h_ref, gain_ref, w_ref, out_ref, *, n_chunk):
    x = h_ref[0]
    ms = jnp.mean(x * x, axis=-1, keepdims=True)
    xn = (x * lax.rsqrt(ms + EPS) * gain_ref[...]).astype(BF16)
    n_out = w_ref.shape[1]
    for c in range(0, n_out, n_chunk):
        res = jnp.dot(xn, w_ref[:, c:c + n_chunk], preferred_element_type=F32)
        for j in range(n_chunk // HEAD_DIM):
            out_ref[c // HEAD_DIM + j, 0] = res[:, j * HEAD_DIM:(j + 1) * HEAD_DIM].astype(BF16)


def _in_proj(h, gain, w):
    bsz, lp, d = h.shape
    n_out = w.shape[1]
    tm = _largest_tile(lp, 528)
    n_groups = n_out // HEAD_DIM
    return pl.pallas_call(
        functools.partial(_in_proj_kernel, n_chunk=512),
        grid=(bsz, lp // tm),
        in_specs=[
            pl.BlockSpec((1, tm, d), lambda b, i: (b, i, 0)),
            pl.BlockSpec((1, d), lambda b, i: (0, 0)),
            pl.BlockSpec((d, n_out), lambda b, i: (0, 0)),
        ],
        out_specs=pl.BlockSpec((n_groups, 1, tm, HEAD_DIM), lambda b, i: (0, b, i, 0)),
        out_shape=jax.ShapeDtypeStruct((n_groups, bsz, lp, HEAD_DIM), BF16),
        compiler_params=pltpu.CompilerParams(
            dimension_semantics=("parallel", "parallel"), vmem_limit_bytes=VMEM_LIMIT),
        name="in_proj",
    )(h, gain.reshape(1, d), w)


def _out_proj_kernel(og_ref, w_ref, h_ref, gain_ref, out_ref):
    n_heads = og_ref.shape[0]
    lhs = jnp.concatenate([og_ref[h, 0] for h in range(n_heads)], axis=1)
    y = jnp.dot(lhs, w_ref[...], preferred_element_type=F32)
    ms = jnp.mean(y * y, axis=-1, keepdims=True)
    out_ref[0] = h_ref[0] + y * lax.rsqrt(ms + EPS) * gain_ref[...]


def _out_proj(og, w, h, gain, rows):
    n_heads, bsz, og_rows, _ = og.shape
    assert og_rows == rows
    d = h.shape[-1]
    tm = _largest_tile(rows, 528)
    return pl.pallas_call(
        _out_proj_kernel,
        grid=(bsz, rows // tm),
        in_specs=[
            pl.BlockSpec((n_heads, 1, tm, HEAD_DIM), lambda b, i: (0, b, i, 0)),
            pl.BlockSpec((d, d), lambda b, i: (0, 0)),
            pl.BlockSpec((1, tm, d), lambda b, i: (b, i, 0)),
            pl.BlockSpec((1, d), lambda b, i: (0, 0)),
        ],
        out_specs=pl.BlockSpec((1, tm, d), lambda b, i: (b, i, 0)),
        out_shape=jax.ShapeDtypeStruct((bsz, rows, d), F32),
        compiler_params=pltpu.CompilerParams(
            dimension_semantics=("parallel", "parallel"), vmem_limit_bytes=VMEM_LIMIT),
        name="out_proj",
    )(og, w, h, gain.reshape(1, d))


N_LEVELS = 8


def _level_table():
    t = jnp.arange(BLOCK)[:, None]
    s = jnp.arange(BLOCK)[None, :]
    x = jnp.bitwise_xor(t, s)
    lvl = jnp.zeros((BLOCK, BLOCK), jnp.int32)
    for k in range(N_LEVELS - 1):
        lvl = jnp.where(x >= (1 << k), k + 1, lvl)
    return jnp.where(s > t, -1, lvl)


def _bcast_row(x, r):
    return jnp.broadcast_to(x[r:r + 1, :], x.shape)


def _hgrn_kernel(q_ref, f_ref, v_ref, g_ref, lb_ref, onorm_ref, lvl_ref, og_ref, st_ref, *,
                 n_heads, lb_row):
    c = pl.program_id(1)

    @pl.when(c == 0)
    def _():
        st_ref[...] = jnp.zeros_like(st_ref)

    n_slab = BLOCK // SUBLANE
    row = lax.broadcasted_iota(jnp.int32, (SUBLANE, LANE), 0)
    not_tail = c != 0
    lvl = lvl_ref[...]
    onorm = onorm_ref[...]

    def head_body(h, carry):
        q = q_ref[h, 0].astype(F32)
        fz = f_ref[h, 0].astype(F32)
        v = v_ref[h, 0]
        gate = g_ref[h, 0].astype(F32)

        lbl = lb_ref[h]
        e_lb = jnp.exp(lbl - jnp.max(lbl, axis=0, keepdims=True))
        lb = (jnp.sum(e_lb[:lb_row + 1], axis=0, keepdims=True)
              / jnp.sum(e_lb, axis=0, keepdims=True))

        e = jnp.exp(-jnp.abs(fz))
        inv = 1.0 / (1.0 + e)
        pos = fz >= 0
        sig = jnp.where(pos, inv, e * inv)
        sig_neg = jnp.where(pos, e * inv, inv)
        f_full = lb + (1.0 - lb) * sig
        k_full = (1.0 - lb) * sig_neg

        f_s, k_s, q_s = [], [], []
        for i in range(n_slab):
            sl = slice(i * SUBLANE, (i + 1) * SUBLANE)
            valid = jnp.logical_or(not_tail, row + i * SUBLANE < N_META)
            f_s.append(jnp.where(valid, f_full[sl], 1.0))
            k_s.append(jnp.where(valid, k_full[sl], 0.0))
            q_s.append(q[sl])

        def operands(p_s, r_s):
            qm = jnp.concatenate([a * b for a, b in zip(q_s, p_s)], axis=0).astype(BF16)
            km = jnp.concatenate([a * b for a, b in zip(k_s, r_s)], axis=0).astype(BF16)
            return qm, km

        def nt_dot(a, b):
            return lax.dot_general(a, b, (((1,), (1,)), ((), ())), preferred_element_type=F32)

        qb = q.astype(BF16)
        kb = jnp.concatenate(k_s, axis=0).astype(BF16)
        a_mat = jnp.where(lvl == 0, nt_dot(qb, kb), 0.0)

        p1 = f_s
        qm = jnp.concatenate([a * b for a, b in zip(q_s, p1)], axis=0).astype(BF16)
        a_mat = jnp.where(lvl == 1, nt_dot(qm, kb), a_mat)
        odd = (row & 1) == 1
        p2 = [s * jnp.where(odd, pltpu.roll(s, 1, 0), 1.0) for s in p1]
        r2 = [jnp.where(odd, 1.0, pltpu.roll(s, SUBLANE - 1, 0)) for s in p1]
        qm, km = operands(p2, r2)
        a_mat = jnp.where(lvl == 2, nt_dot(qm, km), a_mat)
        hi4 = (row & 3) >= 2
        lo8 = row < 4
        p4 = [s * jnp.where(hi4, jnp.where(lo8, _bcast_row(s, 1), _bcast_row(s, 5)), 1.0) for s in p2]
        r4 = [r * jnp.where(hi4, 1.0, jnp.where(lo8, _bcast_row(s, 3), _bcast_row(s, 7)))
              for s, r in zip(p2, r2)]
        qm, km = operands(p4, r4)
        a_mat = jnp.where(lvl == 3, nt_dot(qm, km), a_mat)
        p_s = [s * jnp.where(lo8, 1.0, _bcast_row(s, 3)) for s in p4]
        r_s = [r * jnp.where(lo8, _bcast_row(s, 7), 1.0) for s, r in zip(p4, r4)]

        level = 4
        n_blk_slabs = 1
        while n_blk_slabs < n_slab:
            qm, km = operands(p_s, r_s)
            a_mat = jnp.where(lvl == level, nt_dot(qm, km), a_mat)
            n_blk = n_slab // n_blk_slabs
            tot = [_bcast_row(p_s[(b + 1) * n_blk_slabs - 1], SUBLANE - 1) for b in range(n_blk)]
            new_p, new_r = list(p_s), list(r_s)
            for b in range(n_blk):
                for i in range(b * n_blk_slabs, (b + 1) * n_blk_slabs):
                    if b % 2 == 1:
                        new_p[i] = p_s[i] * tot[b - 1]
                    else:
                        new_r[i] = r_s[i] * tot[b + 1]
            p_s, r_s = new_p, new_r
            n_blk_slabs *= 2
            level += 1

        qm, km = operands(p_s, r_s)
        st = st_ref[h]
        o = jnp.dot(a_mat.astype(BF16), v, preferred_element_type=F32) + nt_dot(qm, st.astype(BF16))
        d_last = p_s[n_slab - 1][SUBLANE - 1:SUBLANE, :]
        kv_t = lax.dot_general(v, km, (((0,), (0,)), ((), ())), preferred_element_type=F32)
        st_ref[h] = st * d_last + kv_t

        ms = jnp.mean(o * o, axis=-1, keepdims=True)
        o = o * lax.rsqrt(ms + EPS) * onorm
        og_ref[h, 0] = (o * (gate * jax.nn.sigmoid(gate))).astype(BF16)
        return carry

    lax.fori_loop(0, n_heads, head_body, 0)


def _hgrn_mixer(proj, lb, out_norm, layer_idx):
    n_groups, bsz, lp, _ = proj.shape
    n_heads = n_groups // 4
    n_chunks = lp // BLOCK
    n_rows = lb.shape[0]
    lb_h = lb.reshape(n_rows, n_heads, HEAD_DIM).transpose(1, 0, 2)

    def blk(group):
        return pl.BlockSpec((n_heads, 1, BLOCK, HEAD_DIM),
                            lambda b, c: (group, b, (c + n_chunks - 1) % n_chunks, 0))

    return pl.pallas_call(
        functools.partial(_hgrn_kernel, n_heads=n_heads, lb_row=layer_idx),
        grid=(bsz, n_chunks),
        in_specs=[
            blk(0), blk(1), blk(2), blk(3),
            pl.BlockSpec((n_heads, n_rows, HEAD_DIM), lambda b, c: (0, 0, 0)),
            pl.BlockSpec((1, HEAD_DIM), lambda b, c: (0, 0)),
            pl.BlockSpec((BLOCK, BLOCK), lambda b, c: (0, 0)),
        ],
        out_specs=blk(0),
        out_shape=jax.ShapeDtypeStruct((n_heads, bsz, lp, HEAD_DIM), BF16),
        scratch_shapes=[pltpu.VMEM((n_heads, HEAD_DIM, HEAD_DIM), F32)],
        compiler_params=pltpu.CompilerParams(
            dimension_semantics=("parallel", "arbitrary"), vmem_limit_bytes=VMEM_LIMIT),
        name="hgrn_mixer",
    )(proj, proj, proj, proj, lb_h, out_norm.reshape(1, HEAD_DIM), _level_table())


def _sb_kernel(q_ref, k_ref, v_ref, g_ref, tri_ref, og_ref, *, tq, n_x, scale):
    i = pl.program_id(2)
    q = (q_ref[0, 0].astype(F32) * (scale * LOG2E)).astype(BF16)
    tri = tri_ref[...]

    def tile(kt, vt, tri_t, mask, carry, acc):
        z = lax.dot_general(q, kt, (((1,), (1,)), ((), ())), preferred_element_type=F32)
        sp = jnp.maximum(z, 0.0) + jnp.log2(1.0 + jnp.exp2(-jnp.abs(z)))
        if mask is not None:
            sp = jnp.where(mask, sp, 0.0)
        cw = jnp.dot(sp.astype(BF16), tri_t, preferred_element_type=F32)
        a = jnp.exp2(z - cw - carry)
        if mask is not None:
            a = jnp.where(mask, a, 0.0)
        acc = acc + jnp.dot(a.astype(BF16), vt, preferred_element_type=F32)
        return carry + cw[:, :1], acc

    carry = jnp.zeros((tq, 1), F32)
    acc = jnp.zeros((tq, HEAD_DIM), F32)

    start = pl.multiple_of(i * tq, tq)
    t_idx = lax.broadcasted_iota(jnp.int32, (tq, tq), 0)
    s_idx = lax.broadcasted_iota(jnp.int32, (tq, tq), 1)
    carry, acc = tile(k_ref[0, 0, pl.ds(start, tq), :], v_ref[0, 0, pl.ds(start, tq), :],
                      tri, s_idx < t_idx, carry, acc)

    def body(step, state):
        carry, acc = state
        off = pl.multiple_of((i - 1 - step) * tq, tq)
        return tile(k_ref[0, 0, pl.ds(off, tq), :], v_ref[0, 0, pl.ds(off, tq), :],
                    tri, None, carry, acc)

    carry, acc = lax.fori_loop(0, i, body, (carry, acc))

    meta_ok = lax.broadcasted_iota(jnp.int32, (tq, BLOCK), 1) < N_META
    carry, acc = tile(k_ref[0, 0, pl.ds(n_x, BLOCK), :], v_ref[0, 0, pl.ds(n_x, BLOCK), :],
                      tri[:BLOCK, :BLOCK], meta_ok, carry, acc)

    gate = g_ref[0, 0].astype(F32)
    og_ref[0, 0] = (acc * (gate * jax.nn.sigmoid(gate))).astype(BF16)


def _sb_mixer(proj, n_x):
    n_groups, bsz, lp, _ = proj.shape
    n_heads = n_groups // 4
    tq = _largest_tile(n_x, 256)
    assert tq >= BLOCK
    tri = (jnp.arange(tq)[:, None] >= jnp.arange(tq)[None, :]).astype(BF16)

    def rows(group):
        return pl.BlockSpec((1, 1, tq, HEAD_DIM), lambda b, h, i: (group * n_heads + h, b, i, 0))

    def full(group):
        return pl.BlockSpec((1, 1, lp, HEAD_DIM), lambda b, h, i: (group * n_heads + h, b, 0, 0))

    return pl.pallas_call(
        functools.partial(_sb_kernel, tq=tq, n_x=n_x, scale=HEAD_DIM ** -0.5),
        grid=(bsz, n_heads, n_x // tq),
        in_specs=[rows(0), full(1), full(2), rows(3),
                  pl.BlockSpec((tq, tq), lambda b, h, i: (0, 0))],
        out_specs=pl.BlockSpec((1, 1, tq, HEAD_DIM), lambda b, h, i: (h, b, i, 0)),
        out_shape=jax.ShapeDtypeStruct((n_heads, bsz, n_x, HEAD_DIM), BF16),
        compiler_params=pltpu.CompilerParams(
            dimension_semantics=("parallel", "parallel", "parallel"), vmem_limit_bytes=VMEM_LIMIT),
        name="sb_mixer",
    )(proj, proj, proj, proj, tri)


def kernel(x, meta_tokens, pre_norm, post_norm, hgrn_w_in, hgrn_lb, hgrn_out_norm, hgrn_w_out,
           sb_w_in, sb_w_out):
    bsz, n_x, d = x.shape
    depth = pre_norm.shape[0]
    assert depth == 2 and meta_tokens.shape[0] == N_META and n_x % BLOCK == 0 and d % HEAD_DIM == 0

    tail = jnp.concatenate([meta_tokens.astype(x.dtype), jnp.zeros((BLOCK - N_META, d), x.dtype)], axis=0)
    h = jnp.concatenate([x, jnp.broadcast_to(tail[None], (bsz, BLOCK, d))], axis=1)
    lp = n_x + BLOCK

    proj = _in_proj(h, pre_norm[0], hgrn_w_in[0].astype(BF16))
    og = _hgrn_mixer(proj, hgrn_lb.astype(F32), hgrn_out_norm[0], 0)
    h = _out_proj(og, hgrn_w_out[0].astype(BF16), h, post_norm[0], lp)

    proj = _in_proj(h, pre_norm[1], sb_w_in[0].astype(BF16))
    og = _sb_mixer(proj, n_x)
    return _out_proj(og, sb_w_out[0].astype(BF16), h, post_norm[1], n_x)
```

```python
import functools
import math

import jax
import jax.numpy as jnp
from jax import lax
from jax.experimental import pallas as pl
from jax.experimental.pallas import tpu as pltpu

LANE = 128
SUBLANE = 8
HEAD_DIM = 128
BLOCK = 128
N_META = 16
EPS = 1e-6
LOG2E = 1.4426950408889634
VMEM_LIMIT = 48 * 1024 * 1024
F32 = jnp.float32
BF16 = jnp.bfloat16


def _largest_tile(n, cap, align=16):
    best = None
    for t in range(align, min(n, cap) + 1, align):
        if n % t == 0:
            best = t
    assert best is not None, (n, cap)
    return best


def _in_proj_kernel(h_ref, gain_ref, w_ref, out_ref, *, n_chunk):
    x = h_ref[0]
    ms = jnp.mean(x * x, axis=-1, keepdims=True)
    xn = (x * lax.rsqrt(ms + EPS) * gain_ref[...]).astype(BF16)
    n_out = w_ref.shape[1]
    for c in range(0, n_out, n_chunk):
        res = jnp.dot(xn, w_ref[:, c:c + n_chunk], preferred_element_type=F32)
        for j in range(n_chunk // HEAD_DIM):
            out_ref[c // HEAD_DIM + j, 0] = res[:, j * HEAD_DIM:(j + 1) * HEAD_DIM].astype(BF16)


def _in_proj(h, gain, w):
    bsz, lp, d = h.shape
    n_out = w.shape[1]
    tm = _largest_tile(lp, 528)
    n_groups = n_out // HEAD_DIM
    return pl.pallas_call(
        functools.partial(_in_proj_kernel, n_chunk=512),
        grid=(bsz, lp // tm),
        in_specs=[
            pl.BlockSpec((1, tm, d), lambda b, i: (b, i, 0)),
            pl.BlockSpec((1, d), lambda b, i: (0, 0)),
            pl.BlockSpec((d, n_out), lambda b, i: (0, 0)),
        ],
        out_specs=pl.BlockSpec((n_groups, 1, tm, HEAD_DIM), lambda b, i: (0, b, i, 0)),
        out_shape=jax.ShapeDtypeStruct((n_groups, bsz, lp, HEAD_DIM), BF16),
        compiler_params=pltpu.CompilerParams(
            dimension_semantics=("parallel", "parallel"), vmem_limit_bytes=VMEM_LIMIT),
        name="in_proj",
    )(h, gain.reshape(1, d), w)


def _out_proj_kernel(og_ref, w_ref, h_ref, gain_ref, out_ref):
    n_heads = og_ref.shape[0]
    lhs = jnp.concatenate([og_ref[h, 0] for h in range(n_heads)], axis=1)
    y = jnp.dot(lhs, w_ref[...], preferred_element_type=F32)
    ms = jnp.mean(y * y, axis=-1, keepdims=True)
    out_ref[0] = h_ref[0] + y * lax.rsqrt(ms + EPS) * gain_ref[...]


def _out_proj(og, w, h, gain, rows):
    n_heads, bsz, og_rows, _ = og.shape
    assert og_rows == rows
    d = h.shape[-1]
    tm = _largest_tile(rows, 528)
    return pl.pallas_call(
        _out_proj_kernel,
        grid=(bsz, rows // tm),
        in_specs=[
            pl.BlockSpec((n_heads, 1, tm, HEAD_DIM), lambda b, i: (0, b, i, 0)),
            pl.BlockSpec((d, d), lambda b, i: (0, 0)),
            pl.BlockSpec((1, tm, d), lambda b, i: (b, i, 0)),
            pl.BlockSpec((1, d), lambda b, i: (0, 0)),
        ],
        out_specs=pl.BlockSpec((1, tm, d), lambda b, i: (b, i, 0)),
        out_shape=jax.ShapeDtypeStruct((bsz, rows, d), F32),
        compiler_params=pltpu.CompilerParams(
            dimension_semantics=("parallel", "parallel"), vmem_limit_bytes=VMEM_LIMIT),
        name="out_proj",
    )(og, w, h, gain.reshape(1, d))


N_LEVELS = 8


def _level_table():
    t = jnp.arange(BLOCK)[:, None]
    s = jnp.arange(BLOCK)[None, :]
    x = jnp.bitwise_xor(t, s)
    lvl = jnp.zeros((BLOCK, BLOCK), jnp.int32)
    for k in range(N_LEVELS - 1):
        lvl = jnp.where(x >= (1 << k), k + 1, lvl)
    return jnp.where(s > t, -1, lvl)


def _bcast_row(x, r):
    return jnp.broadcast_to(x[r:r + 1, :], x.shape)


def _hgrn_kernel(q_ref, f_ref, v_ref, g_ref, lb_ref, onorm_ref, lvl_ref, og_ref, st_ref, *,
                 n_heads, lb_row):
    c = pl.program_id(1)

    @pl.when(c == 0)
    def _():
        st_ref[...] = jnp.zeros_like(st_ref)

    n_slab = BLOCK // SUBLANE
    row = lax.broadcasted_iota(jnp.int32, (SUBLANE, LANE), 0)
    not_tail = c != 0
    lvl = lvl_ref[...]
    onorm = onorm_ref[...]

    def head_body(h, carry):
        q = q_ref[h, 0].astype(F32)
        fz = f_ref[h, 0].astype(F32)
        v = v_ref[h, 0]
        gate = g_ref[h, 0].astype(F32)

        lbl = lb_ref[h]
        e_lb = jnp.exp(lbl - jnp.max(lbl, axis=0, keepdims=True))
        lb = (jnp.sum(e_lb[:lb_row + 1], axis=0, keepdims=True)
              / jnp.sum(e_lb, axis=0, keepdims=True))

        e = jnp.exp(-jnp.abs(fz))
        inv = 1.0 / (1.0 + e)
        pos = fz >= 0
        sig = jnp.where(pos, inv, e * inv)
        sig_neg = jnp.where(pos, e * inv, inv)
        f_full = lb + (1.0 - lb) * sig
        k_full = (1.0 - lb) * sig_neg

        f_s, k_s, q_s = [], [], []
        for i in range(n_slab):
            sl = slice(i * SUBLANE, (i + 1) * SUBLANE)
            valid = jnp.logical_or(not_tail, row + i * SUBLANE < N_META)
            f_s.append(jnp.where(valid, f_full[sl], 1.0))
            k_s.append(jnp.where(valid, k_full[sl], 0.0))
            q_s.append(q[sl])

        def operands(p_s, r_s):
            qm = jnp.concatenate([a * b for a, b in zip(q_s, p_s)], axis=0).astype(BF16)
            km = jnp.concatenate([a * b for a, b in zip(k_s, r_s)], axis=0).astype(BF16)
            return qm, km

        def nt_dot(a, b):
            return lax.dot_general(a, b, (((1,), (1,)), ((), ())), preferred_element_type=F32)

        qb = q.astype(BF16)
        kb = jnp.concatenate(k_s, axis=0).astype(BF16)
        a_mat = jnp.where(lvl == 0, nt_dot(qb, kb), 0.0)

        p1 = f_s
        qm = jnp.concatenate([a * b for a, b in zip(q_s, p1)], axis=0).astype(BF16)
        a_mat = jnp.where(lvl == 1, nt_dot(qm, kb), a_mat)
        odd = (row & 1) == 1
        p2 = [s * jnp.where(odd, pltpu.roll(s, 1, 0), 1.0) for s in p1]
        r2 = [jnp.where(odd, 1.0, pltpu.roll(s, SUBLANE - 1, 0)) for s in p1]
        qm, km = operands(p2, r2)
        a_mat = jnp.where(lvl == 2, nt_dot(qm, km), a_mat)
        hi4 = (row & 3) >= 2
        lo8 = row < 4
        p4 = [s * jnp.where(hi4, jnp.where(lo8, _bcast_row(s, 1), _bcast_row(s, 5)), 1.0) for s in p2]
        r4 = [r * jnp.where(hi4, 1.0, jnp.where(lo8, _bcast_row(s, 3), _bcast_row(s, 7)))
              for s, r in zip(p2, r2)]
        qm, km = operands(p4, r4)
        a_mat = jnp.where(lvl == 3, nt_dot(qm, km), a_mat)
        p_s = [s * jnp.where(lo8, 1.0, _bcast_row(s, 3)) for s in p4]
        r_s = [r * jnp.where(lo8, _bcast_row(s, 7), 1.0) for s, r in zip(p4, r4)]

        level = 4
        n_blk_slabs = 1
        while n_blk_slabs < n_slab:
            qm, km = operands(p_s, r_s)
            a_mat = jnp.where(lvl == level, nt_dot(qm, km), a_mat)
            n_blk = n_slab // n_blk_slabs
            tot = [_bcast_row(p_s[(b + 1) * n_blk_slabs - 1], SUBLANE - 1) for b in range(n_blk)]
            new_p, new_r = list(p_s), list(r_s)
            for b in range(n_blk):
                for i in range(b * n_blk_slabs, (b + 1) * n_blk_slabs):
                    if b % 2 == 1:
                        new_p[i] = p_s[i] * tot[b - 1]
                    else:
                        new_r[i] = r_s[i] * tot[b + 1]
            p_s, r_s = new_p, new_r
            n_blk_slabs *= 2
            level += 1

        qm, km = operands(p_s, r_s)
        st = st_ref[h]
        o = jnp.dot(a_mat.astype(BF16), v, preferred_element_type=F32) + nt_dot(qm, st.astype(BF16))
        d_last = p_s[n_slab - 1][SUBLANE - 1:SUBLANE, :]
        kv_t = lax.dot_general(v, km, (((0,), (0,)), ((), ())), preferred_element_type=F32)
        st_ref[h] = st * d_last + kv_t

        ms = jnp.mean(o * o, axis=-1, keepdims=True)
        o = o * lax.rsqrt(ms + EPS) * onorm
        og_ref[h, 0] = (o * (gate * jax.nn.sigmoid(gate))).astype(BF16)
        return carry

    lax.fori_loop(0, n_heads, head_body, 0, unroll=True)


def _hgrn_mixer(proj, lb, out_norm, layer_idx):
    n_groups, bsz, lp, _ = proj.shape
    n_heads = n_groups // 4
    n_chunks = lp // BLOCK
    n_rows = lb.shape[0]
    lb_h = lb.reshape(n_rows, n_heads, HEAD_DIM).transpose(1, 0, 2)

    def blk(group):
        return pl.BlockSpec((n_heads, 1, BLOCK, HEAD_DIM),
                            lambda b, c: (group, b, (c + n_chunks - 1) % n_chunks, 0))

    return pl.pallas_call(
        functools.partial(_hgrn_kernel, n_heads=n_heads, lb_row=layer_idx),
        grid=(bsz, n_chunks),
        in_specs=[
            blk(0), blk(1), blk(2), blk(3),
            pl.BlockSpec((n_heads, n_rows, HEAD_DIM), lambda b, c: (0, 0, 0)),
            pl.BlockSpec((1, HEAD_DIM), lambda b, c: (0, 0)),
            pl.BlockSpec((BLOCK, BLOCK), lambda b, c: (0, 0)),
        ],
        out_specs=blk(0),
        out_shape=jax.ShapeDtypeStruct((n_heads, bsz, lp, HEAD_DIM), BF16),
        scratch_shapes=[pltpu.VMEM((n_heads, HEAD_DIM, HEAD_DIM), F32)],
        compiler_params=pltpu.CompilerParams(
            dimension_semantics=("parallel", "arbitrary"), vmem_limit_bytes=VMEM_LIMIT),
        name="hgrn_mixer",
    )(proj, proj, proj, proj, lb_h, out_norm.reshape(1, HEAD_DIM), _level_table())


def _sb_kernel(q_ref, k_ref, v_ref, g_ref, tri_ref, og_ref, *, tq, tk, n_x, scale):
    i = pl.program_id(2)
    q = (q_ref[0, 0].astype(F32) * (scale * LOG2E)).astype(BF16)
    tri = tri_ref[...]

    def tile(qt, start, width, tri_t, mask, carry, acc):
        kt = k_ref[0, 0, pl.ds(start, width), :]
        vt = v_ref[0, 0, pl.ds(start, width), :]
        z = lax.dot_general(qt, kt, (((1,), (1,)), ((), ())), preferred_element_type=F32)
        sp = jnp.maximum(z, 0.0) + jnp.log2(1.0 + jnp.exp2(-jnp.abs(z)))
        if mask is not None:
            sp = jnp.where(mask, sp, 0.0)
        cw = jnp.dot(sp.astype(BF16), tri_t, preferred_element_type=F32)
        a = jnp.exp2(z - cw - carry)
        if mask is not None:
            a = jnp.where(mask, a, 0.0)
        acc = acc + jnp.dot(a.astype(BF16), vt, preferred_element_type=F32)
        return carry + cw[:, :1], acc

    n_sub = tq // tk
    base = pl.multiple_of(i * tq, tq)
    strict = (lax.broadcasted_iota(jnp.int32, (tk, tk), 1)
              < lax.broadcasted_iota(jnp.int32, (tk, tk), 0))
    carries, accs = [], []
    for r in range(n_sub):
        qr = q[r * tk:(r + 1) * tk]
        c_r = jnp.zeros((tk, 1), F32)
        a_r = jnp.zeros((tk, HEAD_DIM), F32)
        for sub in range(r, -1, -1):
            c_r, a_r = tile(qr, base + sub * tk, tk, tri, strict if sub == r else None, c_r, a_r)
        carries.append(c_r)
        accs.append(a_r)
    carry = jnp.concatenate(carries, axis=0)
    acc = jnp.concatenate(accs, axis=0)

    def body(step, state):
        carry, acc = state
        off = pl.multiple_of((i - 1 - step) * tq, tq)
        for sub in range(n_sub - 1, -1, -1):
            carry, acc = tile(q, off + sub * tk, tk, tri, None, carry, acc)
        return carry, acc

    carry, acc = lax.fori_loop(0, i, body, (carry, acc))

    meta_ok = lax.broadcasted_iota(jnp.int32, (tq, BLOCK), 1) < N_META
    carry, acc = tile(q, n_x, BLOCK, tri[:BLOCK, :BLOCK], meta_ok, carry, acc)

    gate = g_ref[0, 0].astype(F32)
    og_ref[0, 0] = (acc * (gate * jax.nn.sigmoid(gate))).astype(BF16)


def _sb_mixer(proj, n_x):
    n_groups, bsz, lp, _ = proj.shape
    n_heads = n_groups // 4
    tk = _largest_tile(n_x, 256, BLOCK)
    tq = 2 * tk if n_x % (2 * tk) == 0 else tk
    tri = (jnp.arange(tk)[:, None] >= jnp.arange(tk)[None, :]).astype(BF16)

    def rows(group):
        return pl.BlockSpec((1, 1, tq, HEAD_DIM), lambda b, h, i: (group * n_heads + h, b, i, 0))

    def full(group):
        return pl.BlockSpec((1, 1, lp, HEAD_DIM), lambda b, h, i: (group * n_heads + h, b, 0, 0))

    return pl.pallas_call(
        functools.partial(_sb_kernel, tq=tq, tk=tk, n_x=n_x, scale=HEAD_DIM ** -0.5),
        grid=(bsz, n_heads, n_x // tq),
        in_specs=[rows(0), full(1), full(2), rows(3),
                  pl.BlockSpec((tk, tk), lambda b, h, i: (0, 0))],
        out_specs=pl.BlockSpec((1, 1, tq, HEAD_DIM), lambda b, h, i: (h, b, i, 0)),
        out_shape=jax.ShapeDtypeStruct((n_heads, bsz, n_x, HEAD_DIM), BF16),
        compiler_params=pltpu.CompilerParams(
            dimension_semantics=("parallel", "parallel", "parallel"), vmem_limit_bytes=VMEM_LIMIT),
        name="sb_mixer",
    )(proj, proj, proj, proj, tri)


def kernel(x, meta_tokens, pre_norm, post_norm, hgrn_w_in, hgrn_lb, hgrn_out_norm, hgrn_w_out,
           sb_w_in, sb_w_out):
    bsz, n_x, d = x.shape
    depth = pre_norm.shape[0]
    assert depth == 2 and meta_tokens.shape[0] == N_META and n_x % BLOCK == 0 and d % HEAD_DIM == 0

    tail = jnp.concatenate([meta_tokens.astype(x.dtype), jnp.zeros((BLOCK - N_META, d), x.dtype)], axis=0)
    h = jnp.concatenate([x, jnp.broadcast_to(tail[None], (bsz, BLOCK, d))], axis=1)
    lp = n_x + BLOCK

    proj = _in_proj(h, pre_norm[0], hgrn_w_in[0].astype(BF16))
    og = _hgrn_mixer(proj, hgrn_lb.astype(F32), hgrn_out_norm[0], 0)
    h = _out_proj(og, hgrn_w_out[0].astype(BF16), h, post_norm[0], lp)

    proj = _in_proj(h, pre_norm[1], sb_w_in[0].astype(BF16))
    og = _sb_mixer(proj, n_x)
    return _out_proj(og, sb_w_out[0].astype(BF16), h, post_norm[1], n_x)
```

```python
import functools
import math

import jax
import jax.numpy as jnp
from jax import lax
from jax.experimental import pallas as pl
from jax.experimental.pallas import tpu as pltpu

LANE = 128
SUBLANE = 8
HEAD_DIM = 128
BLOCK = 128
N_META = 16
EPS = 1e-6
LOG2E = 1.4426950408889634
VMEM_LIMIT = 48 * 1024 * 1024
SB_ROW_GROUPS = 4
SB_PIPE_TILES = 4
F32 = jnp.float32
BF16 = jnp.bfloat16


def _largest_tile(n, cap, align=16):
    best = None
    for t in range(align, min(n, cap) + 1, align):
        if n % t == 0:
            best = t
    assert best is not None, (n, cap)
    return best


def _tile_rows(x_ref, tail_ref, n_x):
    rows = x_ref[0]
    if tail_ref is None:
        return rows
    tm = rows.shape[0]
    row = pl.program_id(1) * tm + lax.broadcasted_iota(jnp.int32, (tm, 1), 0)
    return jnp.where(row >= n_x, tail_ref[...], rows)


def _place_tail(tail, tm, n_x):
    off = n_x % tm
    assert off + tail.shape[0] == tm
    return jnp.concatenate([jnp.zeros((off, tail.shape[1]), tail.dtype), tail], axis=0)


def _in_proj_kernel(*refs, n_chunk, n_x, has_tail):
    h_ref, tail_ref, gain_ref, w_ref, out_ref = refs if has_tail else (refs[0], None) + refs[1:]
    x = _tile_rows(h_ref, tail_ref, n_x)
    ms = jnp.mean(x * x, axis=-1, keepdims=True)
    xn = (x * lax.rsqrt(ms + EPS) * gain_ref[...]).astype(BF16)
    n_out = w_ref.shape[1]
    for c in range(0, n_out, n_chunk):
        res = jnp.dot(xn, w_ref[:, c:c + n_chunk], preferred_element_type=F32)
        for j in range(n_chunk // HEAD_DIM):
            out_ref[c // HEAD_DIM + j, 0] = res[:, j * HEAD_DIM:(j + 1) * HEAD_DIM].astype(BF16)


def _in_proj(h, gain, w, tail=None):
    bsz, n_x, d = h.shape
    lp = n_x + (0 if tail is None else tail.shape[0])
    n_out = w.shape[1]
    tm = _largest_tile(lp, 528)
    n_groups = n_out // HEAD_DIM
    const = lambda b, i: (0, 0)
    in_specs = [pl.BlockSpec((1, tm, d), lambda b, i: (b, i, 0))]
    args = [h]
    if tail is not None:
        in_specs.append(pl.BlockSpec((tm, d), const))
        args.append(_place_tail(tail, tm, n_x))
    in_specs += [pl.BlockSpec((1, d), const), pl.BlockSpec((d, n_out), const)]
    args += [gain.reshape(1, d), w]
    return pl.pallas_call(
        functools.partial(_in_proj_kernel, n_chunk=512, n_x=n_x, has_tail=tail is not None),
        grid=(bsz, lp // tm),
        in_specs=in_specs,
        out_specs=pl.BlockSpec((n_groups, 1, tm, HEAD_DIM), lambda b, i: (0, b, i, 0)),
        out_shape=jax.ShapeDtypeStruct((n_groups, bsz, lp, HEAD_DIM), BF16),
        compiler_params=pltpu.CompilerParams(
            dimension_semantics=("parallel", "parallel"), vmem_limit_bytes=VMEM_LIMIT),
        name="in_proj",
    )(*args)


def _out_proj_kernel(*refs, n_x, has_tail):
    og_ref, w_ref, h_ref, tail_ref, gain_ref, out_ref = refs if has_tail else refs[:3] + (None,) + refs[3:]
    n_heads = og_ref.shape[0]
    lhs = jnp.concatenate([og_ref[h, 0] for h in range(n_heads)], axis=1)
    y = jnp.dot(lhs, w_ref[...], preferred_element_type=F32)
    ms = jnp.mean(y * y, axis=-1, keepdims=True)
    out_ref[0] = _tile_rows(h_ref, tail_ref, n_x) + y * lax.rsqrt(ms + EPS) * gain_ref[...]


def _out_proj(og, w, res, gain, tail=None):
    n_heads, bsz, rows, _ = og.shape
    n_x, d = res.shape[1:]
    tm = _largest_tile(rows, 528)
    const = lambda b, i: (0, 0)
    in_specs = [
        pl.BlockSpec((n_heads, 1, tm, HEAD_DIM), lambda b, i: (0, b, i, 0)),
        pl.BlockSpec((d, d), const),
        pl.BlockSpec((1, tm, d), lambda b, i: (b, i, 0)),
    ]
    args = [og, w, res]
    if tail is not None:
        assert rows == n_x + tail.shape[0]
        in_specs.append(pl.BlockSpec((tm, d), const))
        args.append(_place_tail(tail, tm, n_x))
    in_specs.append(pl.BlockSpec((1, d), const))
    args.append(gain.reshape(1, d))
    return pl.pallas_call(
        functools.partial(_out_proj_kernel, n_x=n_x, has_tail=tail is not None),
        grid=(bsz, rows // tm),
        in_specs=in_specs,
        out_specs=pl.BlockSpec((1, tm, d), lambda b, i: (b, i, 0)),
        out_shape=jax.ShapeDtypeStruct((bsz, rows, d), F32),
        compiler_params=pltpu.CompilerParams(
            dimension_semantics=("parallel", "parallel"), vmem_limit_bytes=VMEM_LIMIT),
        name="out_proj",
    )(*args)


N_LEVELS = 8


def _level_table():
    t = jnp.arange(BLOCK)[:, None]
    s = jnp.arange(BLOCK)[None, :]
    x = jnp.bitwise_xor(t, s)
    lvl = jnp.zeros((BLOCK, BLOCK), jnp.int32)
    for k in range(N_LEVELS - 1):
        lvl = jnp.where(x >= (1 << k), k + 1, lvl)
    return jnp.where(s > t, -1, lvl)


def _bcast_row(x, r):
    return jnp.broadcast_to(x[r:r + 1, :], x.shape)


def _hgrn_kernel(q_ref, f_ref, v_ref, g_ref, lb_ref, onorm_ref, lvl_ref, og_ref, st_ref, *,
                 n_heads, lb_row):
    c = pl.program_id(1)

    @pl.when(c == 0)
    def _():
        st_ref[...] = jnp.zeros_like(st_ref)

    n_slab = BLOCK // SUBLANE
    row = lax.broadcasted_iota(jnp.int32, (SUBLANE, LANE), 0)
    not_tail = c != 0
    lvl = lvl_ref[...]
    onorm = onorm_ref[...]

    def head_body(h, carry):
        q = q_ref[h, 0].astype(F32)
        fz = f_ref[h, 0].astype(F32)
        v = v_ref[h, 0]
        gate = g_ref[h, 0].astype(F32)

        lbl = lb_ref[h]
        e_lb = jnp.exp(lbl - jnp.max(lbl, axis=0, keepdims=True))
        lb = (jnp.sum(e_lb[:lb_row + 1], axis=0, keepdims=True)
              / jnp.sum(e_lb, axis=0, keepdims=True))

        e = jnp.exp(-jnp.abs(fz))
        inv = 1.0 / (1.0 + e)
        pos = fz >= 0
        sig = jnp.where(pos, inv, e * inv)
        sig_neg = jnp.where(pos, e * inv, inv)
        f_full = lb + (1.0 - lb) * sig
        k_full = (1.0 - lb) * sig_neg

        f_s, k_s, q_s = [], [], []
        for i in range(n_slab):
            sl = slice(i * SUBLANE, (i + 1) * SUBLANE)
            valid = jnp.logical_or(not_tail, row + i * SUBLANE < N_META)
            f_s.append(jnp.where(valid, f_full[sl], 1.0))
            k_s.append(jnp.where(valid, k_full[sl], 0.0))
            q_s.append(q[sl])

        def operands(p_s, r_s):
            qm = jnp.concatenate([a * b for a, b in zip(q_s, p_s)], axis=0).astype(BF16)
            km = jnp.concatenate([a * b for a, b in zip(k_s, r_s)], axis=0).astype(BF16)
            return qm, km

        def nt_dot(a, b):
            return lax.dot_general(a, b, (((1,), (1,)), ((), ())), preferred_element_type=F32)

        qb = q.astype(BF16)
        kb = jnp.concatenate(k_s, axis=0).astype(BF16)
        a_mat = jnp.where(lvl == 0, nt_dot(qb, kb), 0.0)

        p1 = f_s
        qm = jnp.concatenate([a * b for a, b in zip(q_s, p1)], axis=0).astype(BF16)
        a_mat = jnp.where(lvl == 1, nt_dot(qm, kb), a_mat)
        odd = (row & 1) == 1
        p2 = [s * jnp.where(odd, pltpu.roll(s, 1, 0), 1.0) for s in p1]
        r2 = [jnp.where(odd, 1.0, pltpu.roll(s, SUBLANE - 1, 0)) for s in p1]
        qm, km = operands(p2, r2)
        a_mat = jnp.where(lvl == 2, nt_dot(qm, km), a_mat)
        hi4 = (row & 3) >= 2
        lo8 = row < 4
        p4 = [s * jnp.where(hi4, jnp.where(lo8, _bcast_row(s, 1), _bcast_row(s, 5)), 1.0) for s in p2]
        r4 = [r * jnp.where(hi4, 1.0, jnp.where(lo8, _bcast_row(s, 3), _bcast_row(s, 7)))
              for s, r in zip(p2, r2)]
        qm, km = operands(p4, r4)
        a_mat = jnp.where(lvl == 3, nt_dot(qm, km), a_mat)
        p_s = [s * jnp.where(lo8, 1.0, _bcast_row(s, 3)) for s in p4]
        r_s = [r * jnp.where(lo8, _bcast_row(s, 7), 1.0) for s, r in zip(p4, r4)]

        level = 4
        n_blk_slabs = 1
        while n_blk_slabs < n_slab:
            qm, km = operands(p_s, r_s)
            a_mat = jnp.where(lvl == level, nt_dot(qm, km), a_mat)
            n_blk = n_slab // n_blk_slabs
            tot = [_bcast_row(p_s[(b + 1) * n_blk_slabs - 1], SUBLANE - 1) for b in range(n_blk)]
            new_p, new_r = list(p_s), list(r_s)
            for b in range(n_blk):
                for i in range(b * n_blk_slabs, (b + 1) * n_blk_slabs):
                    if b % 2 == 1:
                        new_p[i] = p_s[i] * tot[b - 1]
                    else:
                        new_r[i] = r_s[i] * tot[b + 1]
            p_s, r_s = new_p, new_r
            n_blk_slabs *= 2
            level += 1

        qm, km = operands(p_s, r_s)
        st = st_ref[h]
        o = jnp.dot(a_mat.astype(BF16), v, preferred_element_type=F32) + nt_dot(qm, st.astype(BF16))
        d_last = p_s[n_slab - 1][SUBLANE - 1:SUBLANE, :]
        kv_t = lax.dot_general(v, km, (((0,), (0,)), ((), ())), preferred_element_type=F32)
        st_ref[h] = st * d_last + kv_t

        ms = jnp.mean(o * o, axis=-1, keepdims=True)
        o = o * lax.rsqrt(ms + EPS) * onorm
        og_ref[h, 0] = (o * (gate * jax.nn.sigmoid(gate))).astype(BF16)
        return carry

    lax.fori_loop(0, n_heads, head_body, 0, unroll=True)


def _hgrn_mixer(proj, lb, out_norm, layer_idx):
    n_groups, bsz, lp, _ = proj.shape
    n_heads = n_groups // 4
    n_chunks = lp // BLOCK
    n_rows = lb.shape[0]
    lb_h = lb.reshape(n_rows, n_heads, HEAD_DIM).transpose(1, 0, 2)

    def blk(group):
        return pl.BlockSpec((n_heads, 1, BLOCK, HEAD_DIM),
                            lambda b, c: (group, b, (c + n_chunks - 1) % n_chunks, 0))

    return pl.pallas_call(
        functools.partial(_hgrn_kernel, n_heads=n_heads, lb_row=layer_idx),
        grid=(bsz, n_chunks),
        in_specs=[
            blk(0), blk(1), blk(2), blk(3),
            pl.BlockSpec((n_heads, n_rows, HEAD_DIM), lambda b, c: (0, 0, 0)),
            pl.BlockSpec((1, HEAD_DIM), lambda b, c: (0, 0)),
            pl.BlockSpec((BLOCK, BLOCK), lambda b, c: (0, 0)),
        ],
        out_specs=blk(0),
        out_shape=jax.ShapeDtypeStruct((n_heads, bsz, lp, HEAD_DIM), BF16),
        scratch_shapes=[pltpu.VMEM((n_heads, HEAD_DIM, HEAD_DIM), F32)],
        compiler_params=pltpu.CompilerParams(
            dimension_semantics=("parallel", "arbitrary"), vmem_limit_bytes=VMEM_LIMIT),
        name="hgrn_mixer",
    )(proj, proj, proj, proj, lb_h, out_norm.reshape(1, HEAD_DIM), _level_table())


def _sb_kernel(q_ref, k_ref, v_ref, g_ref, tri_ref, og_ref, d_buf, r_buf, *, tq, tk, n_x, scale):
    i = pl.program_id(2)
    q = (q_ref[0, 0].astype(F32) * (scale * LOG2E)).astype(BF16)
    tri = tri_ref[...]

    def masked(x, mask):
        if mask is None:
            return x
        rows = mask.shape[0]
        head = jnp.where(mask, x[:rows], 0.0)
        return head if rows == x.shape[0] else jnp.concatenate([head, x[rows:]], axis=0)

    def scores(qt, start, width, tri_t, mask):
        kt = k_ref[0, 0, pl.ds(start, width), :]
        z = lax.dot_general(qt, kt, (((1,), (1,)), ((), ())), preferred_element_type=F32)
        sp = masked(jnp.maximum(z, 0.0) + jnp.log2(1.0 + jnp.exp2(-jnp.abs(z))), mask)
        cw = jnp.dot(sp.astype(BF16), tri_t, preferred_element_type=F32)
        return z - cw, cw[:, :1]

    def attend(d, start, width, mask, carry, acc):
        vt = v_ref[0, 0, pl.ds(start, width), :]
        a = masked(jnp.exp2(d - carry), mask)
        return acc + jnp.dot(a.astype(BF16), vt, preferred_element_type=F32)

    n_sub = tq // tk
    assert n_sub % 2 == 0
    n_full = n_sub * i

    def full_start(n):
        return pl.multiple_of((n_full - 1 - n) * tk, tk)

    def stage_scores(n, slot):
        d, r = scores(q, full_start(n), tk, tri, None)
        d_buf[slot] = d
        r_buf[slot] = r

    def stage_attend(n, slot, carry, acc):
        acc = attend(d_buf[slot], full_start(n), tk, None, carry, acc)
        return carry + r_buf[slot], acc

    base = pl.multiple_of(i * tq, tq)
    strict = (lax.broadcasted_iota(jnp.int32, (tk, tk), 1)
              < lax.broadcasted_iota(jnp.int32, (tk, tk), 0))
    c_diag = jnp.zeros((tq, 1), F32)
    acc_diag = jnp.zeros((tq, HEAD_DIM), F32)
    for j in range(n_sub - 1, -1, -1):
        r0 = j * tk
        d, rs = scores(q[r0:], base + r0, tk, tri, strict)
        acc_j = attend(d, base + r0, tk, strict, c_diag[r0:], acc_diag[r0:])
        c_j = c_diag[r0:] + rs
        if r0:
            acc_j = jnp.concatenate([acc_diag[:r0], acc_j], axis=0)
            c_j = jnp.concatenate([c_diag[:r0], c_j], axis=0)
        acc_diag, c_diag = acc_j, c_j

    meta_ok = lax.broadcasted_iota(jnp.int32, (tq, BLOCK), 1) < N_META
    d_t, _ = scores(q, n_x, BLOCK, tri[:BLOCK, :BLOCK], meta_ok)
    acc_tail = attend(d_t, n_x, BLOCK, meta_ok, jnp.zeros((tq, 1), F32), jnp.zeros((tq, HEAD_DIM), F32))

    n_slots = d_buf.shape[0]
    assert n_sub % n_slots == 0
    zero_state = (jnp.zeros((tq, 1), F32), jnp.zeros((tq, HEAD_DIM), F32))

    def full_tiles():
        for s in range(n_slots):
            stage_scores(s, s)

        def body(k, state):
            carry, acc = state
            for s in range(n_slots):
                carry, acc = stage_attend(n_slots * (k - 1) + s, s, carry, acc)
            for s in range(n_slots):
                stage_scores(n_slots * k + s, s)
            return carry, acc

        carry, acc = lax.fori_loop(1, n_full // n_slots, body, zero_state)
        for s in range(n_slots):
            carry, acc = stage_attend(n_full - n_slots + s, s, carry, acc)
        return carry, acc

    c_full, acc_full = lax.cond(i > 0, full_tiles, lambda: zero_state)

    acc = acc_diag + jnp.exp2(-c_diag) * (acc_full + jnp.exp2(-c_full) * acc_tail)
    gate = g_ref[0, 0].astype(F32)
    og_ref[0, 0] = (acc * (gate * jax.nn.sigmoid(gate))).astype(BF16)


def _sb_mixer(proj, n_x):
    n_groups, bsz, lp, _ = proj.shape
    n_heads = n_groups // 4
    tk = _largest_tile(n_x, 256, BLOCK)
    tq = SB_ROW_GROUPS * tk
    assert n_x % tq == 0
    tri = (jnp.arange(tk)[:, None] >= jnp.arange(tk)[None, :]).astype(BF16)

    def rows(group):
        return pl.BlockSpec((1, 1, tq, HEAD_DIM), lambda b, h, i: (group * n_heads + h, b, i, 0))

    def full(group):
        return pl.BlockSpec((1, 1, lp, HEAD_DIM), lambda b, h, i: (group * n_heads + h, b, 0, 0))

    return pl.pallas_call(
        functools.partial(_sb_kernel, tq=tq, tk=tk, n_x=n_x, scale=HEAD_DIM ** -0.5),
        grid=(bsz, n_heads, n_x // tq),
        in_specs=[rows(0), full(1), full(2), rows(3),
                  pl.BlockSpec((tk, tk), lambda b, h, i: (0, 0))],
        out_specs=pl.BlockSpec((1, 1, tq, HEAD_DIM), lambda b, h, i: (h, b, i, 0)),
        out_shape=jax.ShapeDtypeStruct((n_heads, bsz, n_x, HEAD_DIM), BF16),
        scratch_shapes=[pltpu.VMEM((SB_PIPE_TILES, tq, tk), F32), pltpu.VMEM((SB_PIPE_TILES, tq, 1), F32)],
        compiler_params=pltpu.CompilerParams(
            dimension_semantics=("parallel", "parallel", "parallel"), vmem_limit_bytes=VMEM_LIMIT),
        name="sb_mixer",
    )(proj, proj, proj, proj, tri)


def kernel(x, meta_tokens, pre_norm, post_norm, hgrn_w_in, hgrn_lb, hgrn_out_norm, hgrn_w_out,
           sb_w_in, sb_w_out):
    bsz, n_x, d = x.shape
    depth = pre_norm.shape[0]
    assert depth == 2 and meta_tokens.shape[0] == N_META and n_x % BLOCK == 0 and d % HEAD_DIM == 0

    tail = jnp.concatenate([meta_tokens.astype(x.dtype), jnp.zeros((BLOCK - N_META, d), x.dtype)], axis=0)

    proj = _in_proj(x, pre_norm[0], hgrn_w_in[0].astype(BF16), tail)
    og = _hgrn_mixer(proj, hgrn_lb.astype(F32), hgrn_out_norm[0], 0)
    h = _out_proj(og, hgrn_w_out[0].astype(BF16), x, post_norm[0], tail)

    proj = _in_proj(h, pre_norm[1], sb_w_in[0].astype(BF16))
    og = _sb_mixer(proj, n_x)
    return _out_proj(og, sb_w_out[0].astype(BF16), h, post_norm[1])
```

```python
import functools
import math

import jax
import jax.numpy as jnp
from jax import lax
from jax.experimental import pallas as pl
from jax.experimental.pallas import tpu as pltpu

LANE = 128
SUBLANE = 8
HEAD_DIM = 128
BLOCK = 128
N_META = 16
EPS = 1e-6
LOG2E = 1.4426950408889634
VMEM_LIMIT = 48 * 1024 * 1024
HGRN_BATCH = 2
F32 = jnp.float32
BF16 = jnp.bfloat16


def _largest_tile(n, cap, align=16):
    best = None
    for t in range(align, min(n, cap) + 1, align):
        if n % t == 0:
            best = t
    assert best is not None, (n, cap)
    return best


def _tile_rows(x_ref, tail_ref, n_x):
    rows = x_ref[0]
    if tail_ref is None:
        return rows
    tm = rows.shape[0]
    row = pl.program_id(1) * tm + lax.broadcasted_iota(jnp.int32, (tm, 1), 0)
    return jnp.where(row >= n_x, tail_ref[...], rows)


def _place_tail(tail, tm, n_x):
    off = n_x % tm
    assert off + tail.shape[0] == tm
    return jnp.concatenate([jnp.zeros((off, tail.shape[1]), tail.dtype), tail], axis=0)


def _in_proj_kernel(*refs, n_chunk, n_x, has_tail):
    h_ref, tail_ref, gain_ref, w_ref, out_ref = refs if has_tail else (refs[0], None) + refs[1:]
    x = _tile_rows(h_ref, tail_ref, n_x)
    ms = jnp.mean(x * x, axis=-1, keepdims=True)
    xn = (x * lax.rsqrt(ms + EPS) * gain_ref[...]).astype(BF16)
    n_out = w_ref.shape[1]
    for c in range(0, n_out, n_chunk):
        res = jnp.dot(xn, w_ref[:, c:c + n_chunk], preferred_element_type=F32)
        for j in range(n_chunk // HEAD_DIM):
            out_ref[c // HEAD_DIM + j, 0] = res[:, j * HEAD_DIM:(j + 1) * HEAD_DIM].astype(BF16)


def _in_proj(h, gain, w, tail=None):
    bsz, n_x, d = h.shape
    lp = n_x + (0 if tail is None else tail.shape[0])
    n_out = w.shape[1]
    tm = _largest_tile(lp, 528)
    n_groups = n_out // HEAD_DIM
    const = lambda b, i: (0, 0)
    in_specs = [pl.BlockSpec((1, tm, d), lambda b, i: (b, i, 0))]
    args = [h]
    if tail is not None:
        in_specs.append(pl.BlockSpec((tm, d), const))
        args.append(_place_tail(tail, tm, n_x))
    in_specs += [pl.BlockSpec((1, d), const), pl.BlockSpec((d, n_out), const)]
    args += [gain.reshape(1, d), w]
    return pl.pallas_call(
        functools.partial(_in_proj_kernel, n_chunk=512, n_x=n_x, has_tail=tail is not None),
        grid=(bsz, lp // tm),
        in_specs=in_specs,
        out_specs=pl.BlockSpec((n_groups, 1, tm, HEAD_DIM), lambda b, i: (0, b, i, 0)),
        out_shape=jax.ShapeDtypeStruct((n_groups, bsz, lp, HEAD_DIM), BF16),
        compiler_params=pltpu.CompilerParams(
            dimension_semantics=("parallel", "parallel"), vmem_limit_bytes=VMEM_LIMIT),
        name="in_proj",
    )(*args)


def _out_proj_kernel(*refs, n_x, has_tail):
    og_ref, w_ref, h_ref, tail_ref, gain_ref, out_ref = refs if has_tail else refs[:3] + (None,) + refs[3:]
    n_heads = og_ref.shape[0]
    lhs = jnp.concatenate([og_ref[h, 0] for h in range(n_heads)], axis=1)
    y = jnp.dot(lhs, w_ref[...], preferred_element_type=F32)
    ms = jnp.mean(y * y, axis=-1, keepdims=True)
    out_ref[0] = _tile_rows(h_ref, tail_ref, n_x) + y * lax.rsqrt(ms + EPS) * gain_ref[...]


def _out_proj(og, w, res, gain, tail=None):
    n_heads, bsz, rows, _ = og.shape
    n_x, d = res.shape[1:]
    tm = _largest_tile(rows, 528)
    const = lambda b, i: (0, 0)
    in_specs = [
        pl.BlockSpec((n_heads, 1, tm, HEAD_DIM), lambda b, i: (0, b, i, 0)),
        pl.BlockSpec((d, d), const),
        pl.BlockSpec((1, tm, d), lambda b, i: (b, i, 0)),
    ]
    args = [og, w, res]
    if tail is not None:
        assert rows == n_x + tail.shape[0]
        in_specs.append(pl.BlockSpec((tm, d), const))
        args.append(_place_tail(tail, tm, n_x))
    in_specs.append(pl.BlockSpec((1, d), const))
    args.append(gain.reshape(1, d))
    return pl.pallas_call(
        functools.partial(_out_proj_kernel, n_x=n_x, has_tail=tail is not None),
        grid=(bsz, rows // tm),
        in_specs=in_specs,
        out_specs=pl.BlockSpec((1, tm, d), lambda b, i: (b, i, 0)),
        out_shape=jax.ShapeDtypeStruct((bsz, rows, d), F32),
        compiler_params=pltpu.CompilerParams(
            dimension_semantics=("parallel", "parallel"), vmem_limit_bytes=VMEM_LIMIT),
        name="out_proj",
    )(*args)


N_LEVELS = 8


def _level_table():
    t = jnp.arange(BLOCK)[:, None]
    s = jnp.arange(BLOCK)[None, :]
    x = jnp.bitwise_xor(t, s)
    lvl = jnp.zeros((BLOCK, BLOCK), jnp.int32)
    for k in range(N_LEVELS - 1):
        lvl = jnp.where(x >= (1 << k), k + 1, lvl)
    return jnp.where(s > t, -1, lvl)


def _bcast_row(x, r):
    return jnp.broadcast_to(x[r:r + 1, :], x.shape)


def _hgrn_kernel(q_ref, f_ref, v_ref, g_ref, lb_ref, onorm_ref, lvl_ref, og_ref, st_ref, *,
                 n_heads, n_bat, lb_row):
    c = pl.program_id(1)

    @pl.when(c == 0)
    def _():
        st_ref[...] = jnp.zeros_like(st_ref)

    n_slab = BLOCK // SUBLANE
    row = lax.broadcasted_iota(jnp.int32, (SUBLANE, LANE), 0)
    not_tail = c != 0
    lvl = lvl_ref[...]
    onorm = onorm_ref[...]

    def head_body(bh):
        bi, h = divmod(bh, n_heads)
        q = q_ref[h, bi].astype(F32)
        fz = f_ref[h, bi].astype(F32)
        v = v_ref[h, bi]
        gate = g_ref[h, bi].astype(F32)

        lbl = lb_ref[h]
        e_lb = jnp.exp(lbl - jnp.max(lbl, axis=0, keepdims=True))
        lb = (jnp.sum(e_lb[:lb_row + 1], axis=0, keepdims=True)
              / jnp.sum(e_lb, axis=0, keepdims=True))

        e = jnp.exp(-jnp.abs(fz))
        inv = 1.0 / (1.0 + e)
        pos = fz >= 0
        sig = jnp.where(pos, inv, e * inv)
        sig_neg = jnp.where(pos, e * inv, inv)
        f_full = lb + (1.0 - lb) * sig
        k_full = (1.0 - lb) * sig_neg

        f_s, k_s, q_s = [], [], []
        for i in range(n_slab):
            sl = slice(i * SUBLANE, (i + 1) * SUBLANE)
            valid = jnp.logical_or(not_tail, row + i * SUBLANE < N_META)
            f_s.append(jnp.where(valid, f_full[sl], 1.0))
            k_s.append(jnp.where(valid, k_full[sl], 0.0))
            q_s.append(q[sl])

        def operands(p_s, r_s):
            qm = jnp.concatenate([a * b for a, b in zip(q_s, p_s)], axis=0).astype(BF16)
            km = jnp.concatenate([a * b for a, b in zip(k_s, r_s)], axis=0).astype(BF16)
            return qm, km

        def nt_dot(a, b):
            return lax.dot_general(a, b, (((1,), (1,)), ((), ())), preferred_element_type=F32)

        qb = q_ref[h, bi]
        kb = jnp.concatenate(k_s, axis=0).astype(BF16)
        a_mat = jnp.where(lvl == 0, nt_dot(qb, kb), 0.0)

        p1 = f_s
        qm = jnp.concatenate([a * b for a, b in zip(q_s, p1)], axis=0).astype(BF16)
        a_mat = jnp.where(lvl == 1, nt_dot(qm, kb), a_mat)
        odd = (row & 1) == 1
        p2 = [s * jnp.where(odd, pltpu.roll(s, 1, 0), 1.0) for s in p1]
        r2 = [jnp.where(odd, 1.0, pltpu.roll(s, SUBLANE - 1, 0)) for s in p1]
        qm, km = operands(p2, r2)
        a_mat = jnp.where(lvl == 2, nt_dot(qm, km), a_mat)
        hi4 = (row & 3) >= 2
        lo8 = row < 4
        p4 = [s * jnp.where(hi4, jnp.where(lo8, _bcast_row(s, 1), _bcast_row(s, 5)), 1.0) for s in p2]
        r4 = [r * jnp.where(hi4, 1.0, jnp.where(lo8, _bcast_row(s, 3), _bcast_row(s, 7)))
              for s, r in zip(p2, r2)]
        qm, km = operands(p4, r4)
        a_mat = jnp.where(lvl == 3, nt_dot(qm, km), a_mat)
        p_s = [s * jnp.where(lo8, 1.0, _bcast_row(s, 3)) for s in p4]
        r_s = [r * jnp.where(lo8, _bcast_row(s, 7), 1.0) for s, r in zip(p4, r4)]

        level = 4
        n_blk_slabs = 1
        while n_blk_slabs < n_slab:
            qm, km = operands(p_s, r_s)
            a_mat = jnp.where(lvl == level, nt_dot(qm, km), a_mat)
            n_blk = n_slab // n_blk_slabs
            tot = [_bcast_row(p_s[(b + 1) * n_blk_slabs - 1], SUBLANE - 1) for b in range(n_blk)]
            new_p, new_r = list(p_s), list(r_s)
            for b in range(n_blk):
                for i in range(b * n_blk_slabs, (b + 1) * n_blk_slabs):
                    if b % 2 == 1:
                        new_p[i] = p_s[i] * tot[b - 1]
                    else:
                        new_r[i] = r_s[i] * tot[b + 1]
            p_s, r_s = new_p, new_r
            n_blk_slabs *= 2
            level += 1

        qm, km = operands(p_s, r_s)
        st = st_ref[bi, h]
        o = jnp.dot(a_mat.astype(BF16), v, preferred_element_type=F32) + nt_dot(qm, st.astype(BF16))
        d_last = p_s[n_slab - 1][SUBLANE - 1:SUBLANE, :]
        kv_t = lax.dot_general(v, km, (((0,), (0,)), ((), ())), preferred_element_type=F32)
        st_ref[bi, h] = st * d_last + kv_t

        ms = jnp.mean(o * o, axis=-1, keepdims=True)
        o = o * lax.rsqrt(ms + EPS) * onorm
        og_ref[h, bi] = (o * (gate * jax.nn.sigmoid(gate))).astype(BF16)

    for bh in range(n_bat * n_heads):
        head_body(bh)


def _hgrn_mixer(proj, lb, out_norm, layer_idx):
    n_groups, bsz, lp, _ = proj.shape
    n_heads = n_groups // 4
    n_chunks = lp // BLOCK
    n_rows = lb.shape[0]
    n_bat = HGRN_BATCH if bsz % HGRN_BATCH == 0 else 1
    lb_h = lb.reshape(n_rows, n_heads, HEAD_DIM).transpose(1, 0, 2)

    def blk(group):
        return pl.BlockSpec((n_heads, n_bat, BLOCK, HEAD_DIM),
                            lambda b, c: (group, b, (c + n_chunks - 1) % n_chunks, 0))

    return pl.pallas_call(
        functools.partial(_hgrn_kernel, n_heads=n_heads, n_bat=n_bat, lb_row=layer_idx),
        grid=(bsz // n_bat, n_chunks),
        in_specs=[
            blk(0), blk(1), blk(2), blk(3),
            pl.BlockSpec((n_heads, n_rows, HEAD_DIM), lambda b, c: (0, 0, 0)),
            pl.BlockSpec((1, HEAD_DIM), lambda b, c: (0, 0)),
            pl.BlockSpec((BLOCK, BLOCK), lambda b, c: (0, 0)),
        ],
        out_specs=blk(0),
        out_shape=jax.ShapeDtypeStruct((n_heads, bsz, lp, HEAD_DIM), BF16),
        scratch_shapes=[pltpu.VMEM((n_bat, n_heads, HEAD_DIM, HEAD_DIM), F32)],
        compiler_params=pltpu.CompilerParams(
            dimension_semantics=("parallel", "arbitrary"), vmem_limit_bytes=VMEM_LIMIT),
        name="hgrn_mixer",
    )(proj, proj, proj, proj, lb_h, out_norm.reshape(1, HEAD_DIM), _level_table())


def _sb_kernel(q_ref, k_ref, v_ref, g_ref, tri_ref, og_ref, *, tk, n_x, scale):
    q = (q_ref[0, 0].astype(F32) * (scale * LOG2E)).astype(BF16)
    tri = tri_ref[...]

    def masked(x, mask):
        rows = mask.shape[0]
        head = jnp.where(mask, x[:rows], 0.0)
        return head if rows == x.shape[0] else jnp.concatenate([head, x[rows:]], axis=0)

    def tile(qt, start, width, tri_t, mask, carry, acc):
        kt = k_ref[0, 0, start:start + width, :]
        vt = v_ref[0, 0, start:start + width, :]
        z = lax.dot_general(qt, kt, (((1,), (1,)), ((), ())), preferred_element_type=F32)
        sp = masked(jnp.maximum(z, 0.0) + jnp.log2(1.0 + jnp.exp2(-jnp.abs(z))), mask)
        cw = jnp.dot(sp.astype(BF16), tri_t, preferred_element_type=F32)
        a = masked(jnp.exp2(z - cw - carry), mask)
        return carry + cw[:, :1], acc + jnp.dot(a.astype(BF16), vt, preferred_element_type=F32)

    strict = (lax.broadcasted_iota(jnp.int32, (tk, tk), 1)
              < lax.broadcasted_iota(jnp.int32, (tk, tk), 0))
    carry = jnp.zeros((n_x, 1), F32)
    acc = jnp.zeros((n_x, HEAD_DIM), F32)
    for j in range(n_x // tk - 1, -1, -1):
        r0 = j * tk
        c_j, acc_j = tile(q[r0:], r0, tk, tri, strict, carry[r0:], acc[r0:])
        if r0:
            c_j = jnp.concatenate([carry[:r0], c_j], axis=0)
            acc_j = jnp.concatenate([acc[:r0], acc_j], axis=0)
        carry, acc = c_j, acc_j

    meta_ok = lax.broadcasted_iota(jnp.int32, (n_x, BLOCK), 1) < N_META
    _, acc_tail = tile(q, n_x, BLOCK, tri[:BLOCK, :BLOCK], meta_ok, jnp.zeros((n_x, 1), F32),
                       jnp.zeros((n_x, HEAD_DIM), F32))

    acc = acc + jnp.exp2(-carry) * acc_tail
    gate = g_ref[0, 0].astype(F32)
    og_ref[0, 0] = (acc * (gate * jax.nn.sigmoid(gate))).astype(BF16)


def _sb_mixer(proj, n_x):
    n_groups, bsz, lp, _ = proj.shape
    n_heads = n_groups // 4
    tk = _largest_tile(n_x, 256, BLOCK)
    tri = (jnp.arange(tk)[:, None] >= jnp.arange(tk)[None, :]).astype(BF16)

    def rows(group):
        return pl.BlockSpec((1, 1, n_x, HEAD_DIM), lambda b, h: (group * n_heads + h, b, 0, 0))

    def full(group):
        return pl.BlockSpec((1, 1, lp, HEAD_DIM), lambda b, h: (group * n_heads + h, b, 0, 0))

    return pl.pallas_call(
        functools.partial(_sb_kernel, tk=tk, n_x=n_x, scale=HEAD_DIM ** -0.5),
        grid=(bsz, n_heads),
        in_specs=[rows(0), full(1), full(2), rows(3), pl.BlockSpec((tk, tk), lambda b, h: (0, 0))],
        out_specs=pl.BlockSpec((1, 1, n_x, HEAD_DIM), lambda b, h: (h, b, 0, 0)),
        out_shape=jax.ShapeDtypeStruct((n_heads, bsz, n_x, HEAD_DIM), BF16),
        compiler_params=pltpu.CompilerParams(
            dimension_semantics=("parallel", "parallel"), vmem_limit_bytes=VMEM_LIMIT),
        name="sb_mixer",
    )(proj, proj, proj, proj, tri)


def kernel(x, meta_tokens, pre_norm, post_norm, hgrn_w_in, hgrn_lb, hgrn_out_norm, hgrn_w_out,
           sb_w_in, sb_w_out):
    bsz, n_x, d = x.shape
    depth = pre_norm.shape[0]
    assert depth == 2 and meta_tokens.shape[0] == N_META and n_x % BLOCK == 0 and d % HEAD_DIM == 0

    tail = jnp.concatenate([meta_tokens.astype(x.dtype), jnp.zeros((BLOCK - N_META, d), x.dtype)], axis=0)

    proj = _in_proj(x, pre_norm[0], hgrn_w_in[0].astype(BF16), tail)
    og = _hgrn_mixer(proj, hgrn_lb.astype(F32), hgrn_out_norm[0], 0)
    h = _out_proj(og, hgrn_w_out[0].astype(BF16), x, post_norm[0], tail)

    proj = _in_proj(h, pre_norm[1], sb_w_in[0].astype(BF16))
    og = _sb_mixer(proj, n_x)
    return _out_proj(og, sb_w_out[0].astype(BF16), h, post_norm[1])
```

```python
import functools
import math

import jax
import jax.numpy as jnp
from jax import lax
from jax.experimental import pallas as pl
from jax.experimental.pallas import tpu as pltpu

LANE = 128
SUBLANE = 8
HEAD_DIM = 128
BLOCK = 128
N_META = 16
EPS = 1e-6
LOG2E = 1.4426950408889634
VMEM_LIMIT = 48 * 1024 * 1024
HGRN_BATCH = 2
PROJ_CHUNK = 512
F32 = jnp.float32
BF16 = jnp.bfloat16


def _largest_tile(n, cap, align=16):
    best = None
    for t in range(align, min(n, cap) + 1, align):
        if n % t == 0:
            best = t
    assert best is not None, (n, cap)
    return best


def _tile_rows(x_ref, tail_ref, n_x):
    rows = x_ref[0]
    if tail_ref is None:
        return rows
    tm = rows.shape[0]
    row = pl.program_id(1) * tm + lax.broadcasted_iota(jnp.int32, (tm, 1), 0)
    return jnp.where(row >= n_x, tail_ref[...], rows)


def _place_tail(tail, tm, n_x):
    off = n_x % tm
    assert off + tail.shape[0] == tm
    return jnp.concatenate([jnp.zeros((off, tail.shape[1]), tail.dtype), tail], axis=0)


def _project(x, gain_ref, w_ref, out_ref):
    ms = jnp.mean(x * x, axis=-1, keepdims=True)
    xn = (x * lax.rsqrt(ms + EPS) * gain_ref[...]).astype(BF16)
    n_out = w_ref.shape[1]
    for c in range(0, n_out, PROJ_CHUNK):
        res = jnp.dot(xn, w_ref[:, c:c + PROJ_CHUNK], preferred_element_type=F32)
        for j in range(PROJ_CHUNK // HEAD_DIM):
            out_ref[c // HEAD_DIM + j, 0] = res[:, j * HEAD_DIM:(j + 1) * HEAD_DIM].astype(BF16)


def _in_proj_kernel(*refs, n_x, has_tail):
    h_ref, tail_ref, gain_ref, w_ref, out_ref = refs if has_tail else (refs[0], None) + refs[1:]
    _project(_tile_rows(h_ref, tail_ref, n_x), gain_ref, w_ref, out_ref)


def _in_proj(h, gain, w, tail=None):
    bsz, n_x, d = h.shape
    lp = n_x + (0 if tail is None else tail.shape[0])
    n_out = w.shape[1]
    tm = _largest_tile(lp, 528)
    n_groups = n_out // HEAD_DIM
    const = lambda b, i: (0, 0)
    in_specs = [pl.BlockSpec((1, tm, d), lambda b, i: (b, i, 0))]
    args = [h]
    if tail is not None:
        in_specs.append(pl.BlockSpec((tm, d), const))
        args.append(_place_tail(tail, tm, n_x))
    in_specs += [pl.BlockSpec((1, d), const), pl.BlockSpec((d, n_out), const)]
    args += [gain.reshape(1, d), w]
    return pl.pallas_call(
        functools.partial(_in_proj_kernel, n_x=n_x, has_tail=tail is not None),
        grid=(bsz, lp // tm),
        in_specs=in_specs,
        out_specs=pl.BlockSpec((n_groups, 1, tm, HEAD_DIM), lambda b, i: (0, b, i, 0)),
        out_shape=jax.ShapeDtypeStruct((n_groups, bsz, lp, HEAD_DIM), BF16),
        compiler_params=pltpu.CompilerParams(
            dimension_semantics=("parallel", "parallel"), vmem_limit_bytes=VMEM_LIMIT),
        name="in_proj",
    )(*args)


def _out_proj_kernel(*refs, n_x, has_tail, has_next):
    refs = list(refs)
    og_ref, w_ref, h_ref = refs[:3]
    tail_ref = refs.pop(3) if has_tail else None
    gain_ref = refs[3]
    next_refs = (refs.pop(4), refs.pop(4)) if has_next else None
    out_ref = refs[4]
    n_heads = og_ref.shape[0]
    lhs = jnp.concatenate([og_ref[h, 0] for h in range(n_heads)], axis=1)
    y = jnp.dot(lhs, w_ref[...], preferred_element_type=F32)
    ms = jnp.mean(y * y, axis=-1, keepdims=True)
    h_new = _tile_rows(h_ref, tail_ref, n_x) + y * lax.rsqrt(ms + EPS) * gain_ref[...]
    out_ref[0] = h_new
    if has_next:
        _project(h_new, next_refs[0], next_refs[1], refs[5])


def _out_proj(og, w, res, gain, tail=None, next_proj=None):
    n_heads, bsz, rows, _ = og.shape
    n_x, d = res.shape[1:]
    tm = _largest_tile(rows, 528)
    const = lambda b, i: (0, 0)
    in_specs = [
        pl.BlockSpec((n_heads, 1, tm, HEAD_DIM), lambda b, i: (0, b, i, 0)),
        pl.BlockSpec((d, d), const),
        pl.BlockSpec((1, tm, d), lambda b, i: (b, i, 0)),
    ]
    args = [og, w, res]
    if tail is not None:
        assert rows == n_x + tail.shape[0]
        in_specs.append(pl.BlockSpec((tm, d), const))
        args.append(_place_tail(tail, tm, n_x))
    in_specs.append(pl.BlockSpec((1, d), const))
    args.append(gain.reshape(1, d))
    out_specs = [pl.BlockSpec((1, tm, d), lambda b, i: (b, i, 0))]
    out_shape = [jax.ShapeDtypeStruct((bsz, rows, d), F32)]
    if next_proj is not None:
        n_gain, n_w = next_proj
        n_groups = n_w.shape[1] // HEAD_DIM
        in_specs += [pl.BlockSpec((1, d), const), pl.BlockSpec(n_w.shape, const)]
        args += [n_gain.reshape(1, d), n_w]
        out_specs.append(pl.BlockSpec((n_groups, 1, tm, HEAD_DIM), lambda b, i: (0, b, i, 0)))
        out_shape.append(jax.ShapeDtypeStruct((n_groups, bsz, rows, HEAD_DIM), BF16))
    outs = pl.pallas_call(
        functools.partial(_out_proj_kernel, n_x=n_x, has_tail=tail is not None,
                          has_next=next_proj is not None),
        grid=(bsz, rows // tm),
        in_specs=in_specs,
        out_specs=out_specs,
        out_shape=out_shape,
        compiler_params=pltpu.CompilerParams(
            dimension_semantics=("parallel", "parallel"), vmem_limit_bytes=VMEM_LIMIT),
        name="out_proj",
    )(*args)
    return outs if next_proj is not None else outs[0]


N_LEVELS = 8


def _level_table():
    t = jnp.arange(BLOCK)[:, None]
    s = jnp.arange(BLOCK)[None, :]
    x = jnp.bitwise_xor(t, s)
    lvl = jnp.zeros((BLOCK, BLOCK), jnp.int32)
    for k in range(N_LEVELS - 1):
        lvl = jnp.where(x >= (1 << k), k + 1, lvl)
    return jnp.where(s > t, -1, lvl)


def _bcast_row(x, r):
    return jnp.broadcast_to(x[r:r + 1, :], x.shape)


def _hgrn_kernel(q_ref, f_ref, v_ref, g_ref, lb_ref, onorm_ref, lvl_ref, og_ref, st_ref, *,
                 n_heads, n_bat, lb_row):
    c = pl.program_id(1)

    @pl.when(c == 0)
    def _():
        st_ref[...] = jnp.zeros_like(st_ref)

    n_slab = BLOCK // SUBLANE
    row = lax.broadcasted_iota(jnp.int32, (SUBLANE, LANE), 0)
    not_tail = c != 0
    lvl = lvl_ref[...]
    onorm = onorm_ref[...]

    def head_body(bh):
        bi, h = divmod(bh, n_heads)
        q = q_ref[h, bi].astype(F32)
        fz = f_ref[h, bi].astype(F32)
        v = v_ref[h, bi]
        gate = g_ref[h, bi].astype(F32)

        lbl = lb_ref[h]
        e_lb = jnp.exp(lbl - jnp.max(lbl, axis=0, keepdims=True))
        lb = (jnp.sum(e_lb[:lb_row + 1], axis=0, keepdims=True)
              / jnp.sum(e_lb, axis=0, keepdims=True))

        e = jnp.exp(-jnp.abs(fz))
        inv = 1.0 / (1.0 + e)
        pos = fz >= 0
        sig = jnp.where(pos, inv, e * inv)
        sig_neg = jnp.where(pos, e * inv, inv)
        f_full = lb + (1.0 - lb) * sig
        k_full = (1.0 - lb) * sig_neg

        f_s, k_s, q_s = [], [], []
        for i in range(n_slab):
            sl = slice(i * SUBLANE, (i + 1) * SUBLANE)
            valid = jnp.logical_or(not_tail, row + i * SUBLANE < N_META)
            f_s.append(jnp.where(valid, f_full[sl], 1.0))
            k_s.append(jnp.where(valid, k_full[sl], 0.0))
            q_s.append(q[sl])

        def operands(p_s, r_s):
            qm = jnp.concatenate([a * b for a, b in zip(q_s, p_s)], axis=0).astype(BF16)
            km = jnp.concatenate([a * b for a, b in zip(k_s, r_s)], axis=0).astype(BF16)
            return qm, km

        def nt_dot(a, b):
            return lax.dot_general(a, b, (((1,), (1,)), ((), ())), preferred_element_type=F32)

        lvl_s = [lvl[i * SUBLANE:(i + 1) * SUBLANE] for i in range(n_slab)]

        def select_rows(a_s, level, res, slabs):
            for n, i in enumerate(slabs):
                a_s[i] = jnp.where(lvl_s[i] == level, res[n * SUBLANE:(n + 1) * SUBLANE], a_s[i])

        all_slabs = list(range(n_slab))
        qb = q_ref[h, bi]
        kb = jnp.concatenate(k_s, axis=0).astype(BF16)
        a_s = [jnp.zeros((SUBLANE, LANE), F32)] * n_slab
        select_rows(a_s, 0, nt_dot(qb, kb), all_slabs)
        p1 = f_s
        qm = jnp.concatenate([a * b for a, b in zip(q_s, p1)], axis=0).astype(BF16)
        select_rows(a_s, 1, nt_dot(qm, kb), all_slabs)
        odd = (row & 1) == 1
        p2 = [s * jnp.where(odd, pltpu.roll(s, 1, 0), 1.0) for s in p1]
        r2 = [jnp.where(odd, 1.0, pltpu.roll(s, SUBLANE - 1, 0)) for s in p1]
        select_rows(a_s, 2, nt_dot(*operands(p2, r2)), all_slabs)
        hi4 = (row & 3) >= 2
        lo8 = row < 4
        p4 = [s * jnp.where(hi4, jnp.where(lo8, _bcast_row(s, 1), _bcast_row(s, 5)), 1.0) for s in p2]
        r4 = [r * jnp.where(hi4, 1.0, jnp.where(lo8, _bcast_row(s, 3), _bcast_row(s, 7)))
              for s, r in zip(p2, r2)]
        select_rows(a_s, 3, nt_dot(*operands(p4, r4)), all_slabs)
        p_s = [s * jnp.where(lo8, 1.0, _bcast_row(s, 3)) for s in p4]
        r_s = [r * jnp.where(lo8, _bcast_row(s, 7), 1.0) for s, r in zip(p4, r4)]

        level = 4
        n_blk_slabs = 1
        while n_blk_slabs < n_slab:
            upper = [i for i in range(n_slab) if (i // n_blk_slabs) % 2 == 1]
            qm = jnp.concatenate([q_s[i] * p_s[i] for i in upper], axis=0).astype(BF16)
            km = jnp.concatenate([k_s[i] if i in upper else k_s[i] * r_s[i] for i in range(n_slab)],
                                 axis=0).astype(BF16)
            select_rows(a_s, level, nt_dot(qm, km), upper)
            n_blk = n_slab // n_blk_slabs
            tot = [_bcast_row(p_s[(b + 1) * n_blk_slabs - 1], SUBLANE - 1) for b in range(n_blk)]
            new_p, new_r = list(p_s), list(r_s)
            for b in range(n_blk):
                for i in range(b * n_blk_slabs, (b + 1) * n_blk_slabs):
                    if b % 2 == 1:
                        new_p[i] = p_s[i] * tot[b - 1]
                    else:
                        new_r[i] = r_s[i] * tot[b + 1]
            p_s, r_s = new_p, new_r
            n_blk_slabs *= 2
            level += 1
        a_mat = jnp.concatenate(a_s, axis=0)

        qm, km = operands(p_s, r_s)
        st = st_ref[bi, h]
        o = jnp.dot(a_mat.astype(BF16), v, preferred_element_type=F32) + nt_dot(qm, st.astype(BF16))
        d_last = p_s[n_slab - 1][SUBLANE - 1:SUBLANE, :]
        kv_t = lax.dot_general(v, km, (((0,), (0,)), ((), ())), preferred_element_type=F32)
        st_ref[bi, h] = st * d_last + kv_t

        ms = jnp.mean(o * o, axis=-1, keepdims=True)
        o = o * lax.rsqrt(ms + EPS) * onorm
        og_ref[h, bi] = (o * (gate * jax.nn.sigmoid(gate))).astype(BF16)

    for bh in range(n_bat * n_heads):
        head_body(bh)


def _hgrn_mixer(proj, lb, out_norm, layer_idx):
    n_groups, bsz, lp, _ = proj.shape
    n_heads = n_groups // 4
    n_chunks = lp // BLOCK
    n_rows = lb.shape[0]
    n_bat = HGRN_BATCH if bsz % HGRN_BATCH == 0 else 1
    lb_h = lb.reshape(n_rows, n_heads, HEAD_DIM).transpose(1, 0, 2)

    def blk(group):
        return pl.BlockSpec((n_heads, n_bat, BLOCK, HEAD_DIM),
                            lambda b, c: (group, b, (c + n_chunks - 1) % n_chunks, 0))

    return pl.pallas_call(
        functools.partial(_hgrn_kernel, n_heads=n_heads, n_bat=n_bat, lb_row=layer_idx),
        grid=(bsz // n_bat, n_chunks),
        in_specs=[
            blk(0), blk(1), blk(2), blk(3),
            pl.BlockSpec((n_heads, n_rows, HEAD_DIM), lambda b, c: (0, 0, 0)),
            pl.BlockSpec((1, HEAD_DIM), lambda b, c: (0, 0)),
            pl.BlockSpec((BLOCK, BLOCK), lambda b, c: (0, 0)),
        ],
        out_specs=blk(0),
        out_shape=jax.ShapeDtypeStruct((n_heads, bsz, lp, HEAD_DIM), BF16),
        scratch_shapes=[pltpu.VMEM((n_bat, n_heads, HEAD_DIM, HEAD_DIM), F32)],
        compiler_params=pltpu.CompilerParams(
            dimension_semantics=("parallel", "arbitrary"), vmem_limit_bytes=VMEM_LIMIT),
        name="hgrn_mixer",
    )(proj, proj, proj, proj, lb_h, out_norm.reshape(1, HEAD_DIM), _level_table())


def _sb_kernel(q_ref, k_ref, v_ref, g_ref, tri_ref, og_ref, *, tk, n_x, scale):
    q = (q_ref[0, 0].astype(F32) * (scale * LOG2E)).astype(BF16)
    tri = tri_ref[...]

    def masked(x, mask):
        rows = mask.shape[0]
        head = jnp.where(mask, x[:rows], 0.0)
        return head if rows == x.shape[0] else jnp.concatenate([head, x[rows:]], axis=0)

    def tile(qt, start, width, tri_t, mask, carry, acc):
        kt = k_ref[0, 0, start:start + width, :]
        vt = v_ref[0, 0, start:start + width, :]
        z = lax.dot_general(qt, kt, (((1,), (1,)), ((), ())), preferred_element_type=F32)
        sp = masked(jnp.maximum(z, 0.0) + jnp.log2(1.0 + jnp.exp2(-jnp.abs(z))), mask)
        cw = jnp.dot(sp.astype(BF16), tri_t, preferred_element_type=F32)
        a = masked(jnp.exp2(z - cw - carry), mask)
        return carry + cw[:, :1], acc + jnp.dot(a.astype(BF16), vt, preferred_element_type=F32)

    strict = (lax.broadcasted_iota(jnp.int32, (tk, tk), 1)
              < lax.broadcasted_iota(jnp.int32, (tk, tk), 0))
    carry = jnp.zeros((n_x, 1), F32)
    acc = jnp.zeros((n_x, HEAD_DIM), F32)
    for j in range(n_x // tk - 1, -1, -1):
        r0 = j * tk
        c_j, acc_j = tile(q[r0:], r0, tk, tri, strict, carry[r0:], acc[r0:])
        if r0:
            c_j = jnp.concatenate([carry[:r0], c_j], axis=0)
            acc_j = jnp.concatenate([acc[:r0], acc_j], axis=0)
        carry, acc = c_j, acc_j

    meta_ok = lax.broadcasted_iota(jnp.int32, (n_x, BLOCK), 1) < N_META
    _, acc_tail = tile(q, n_x, BLOCK, tri[:BLOCK, :BLOCK], meta_ok, jnp.zeros((n_x, 1), F32),
                       jnp.zeros((n_x, HEAD_DIM), F32))

    acc = acc + jnp.exp2(-carry) * acc_tail
    gate = g_ref[0, 0].astype(F32)
    og_ref[0, 0] = (acc * (gate * jax.nn.sigmoid(gate))).astype(BF16)


def _sb_mixer(proj, n_x):
    n_groups, bsz, lp, _ = proj.shape
    n_heads = n_groups // 4
    tk = _largest_tile(n_x, 256, BLOCK)
    tri = (jnp.arange(tk)[:, None] >= jnp.arange(tk)[None, :]).astype(BF16)

    def rows(group):
        return pl.BlockSpec((1, 1, n_x, HEAD_DIM), lambda b, h: (group * n_heads + h, b, 0, 0))

    def full(group):
        return pl.BlockSpec((1, 1, lp, HEAD_DIM), lambda b, h: (group * n_heads + h, b, 0, 0))

    return pl.pallas_call(
        functools.partial(_sb_kernel, tk=tk, n_x=n_x, scale=HEAD_DIM ** -0.5),
        grid=(bsz, n_heads),
        in_specs=[rows(0), full(1), full(2), rows(3), pl.BlockSpec((tk, tk), lambda b, h: (0, 0))],
        out_specs=pl.BlockSpec((1, 1, n_x, HEAD_DIM), lambda b, h: (h, b, 0, 0)),
        out_shape=jax.ShapeDtypeStruct((n_heads, bsz, n_x, HEAD_DIM), BF16),
        compiler_params=pltpu.CompilerParams(
            dimension_semantics=("parallel", "parallel"), vmem_limit_bytes=VMEM_LIMIT),
        name="sb_mixer",
    )(proj, proj, proj, proj, tri)


def kernel(x, meta_tokens, pre_norm, post_norm, hgrn_w_in, hgrn_lb, hgrn_out_norm, hgrn_w_out,
           sb_w_in, sb_w_out):
    bsz, n_x, d = x.shape
    depth = pre_norm.shape[0]
    assert depth == 2 and meta_tokens.shape[0] == N_META and n_x % BLOCK == 0 and d % HEAD_DIM == 0

    tail = jnp.concatenate([meta_tokens.astype(x.dtype), jnp.zeros((BLOCK - N_META, d), x.dtype)], axis=0)

    proj = _in_proj(x, pre_norm[0], hgrn_w_in[0].astype(BF16), tail)
    og = _hgrn_mixer(proj, hgrn_lb.astype(F32), hgrn_out_norm[0], 0)
    h, proj = _out_proj(og, hgrn_w_out[0].astype(BF16), x, post_norm[0], tail,
                        next_proj=(pre_norm[1], sb_w_in[0].astype(BF16)))

    og = _sb_mixer(proj, n_x)
    return _out_proj(og, sb_w_out[0].astype(BF16), h, post_norm[1])
```

```python
import functools
import math

import jax
import jax.numpy as jnp
from jax import lax
from jax.experimental import pallas as pl
from jax.experimental.pallas import tpu as pltpu

LANE = 128
SUBLANE = 8
HEAD_DIM = 128
BLOCK = 128
N_META = 16
EPS = 1e-6
LOG2E = 1.4426950408889634
VMEM_LIMIT = 48 * 1024 * 1024
HGRN_BATCH = 8
PROJ_CHUNK = 512
F32 = jnp.float32
BF16 = jnp.bfloat16


def _largest_tile(n, cap, align=16):
    best = None
    for t in range(align, min(n, cap) + 1, align):
        if n % t == 0:
            best = t
    assert best is not None, (n, cap)
    return best


def _tile_rows(x_ref, tail_ref, n_x):
    rows = x_ref[0]
    if tail_ref is None:
        return rows
    tm = rows.shape[0]
    row = pl.program_id(1) * tm + lax.broadcasted_iota(jnp.int32, (tm, 1), 0)
    return jnp.where(row >= n_x, tail_ref[...], rows)


def _place_tail(tail, tm, n_x):
    off = n_x % tm
    assert off + tail.shape[0] == tm
    return jnp.concatenate([jnp.zeros((off, tail.shape[1]), tail.dtype), tail], axis=0)


def _project(x, gain_ref, w_ref, out_ref):
    ms = jnp.mean(x * x, axis=-1, keepdims=True)
    xn = (x * lax.rsqrt(ms + EPS) * gain_ref[...]).astype(BF16)
    n_out = w_ref.shape[1]
    for c in range(0, n_out, PROJ_CHUNK):
        res = jnp.dot(xn, w_ref[:, c:c + PROJ_CHUNK], preferred_element_type=F32)
        for j in range(PROJ_CHUNK // HEAD_DIM):
            out_ref[c // HEAD_DIM + j, 0] = res[:, j * HEAD_DIM:(j + 1) * HEAD_DIM].astype(BF16)


def _in_proj_kernel(*refs, n_x, has_tail):
    h_ref, tail_ref, gain_ref, w_ref, out_ref = refs if has_tail else (refs[0], None) + refs[1:]
    _project(_tile_rows(h_ref, tail_ref, n_x), gain_ref, w_ref, out_ref)


def _in_proj(h, gain, w, tail=None):
    bsz, n_x, d = h.shape
    lp = n_x + (0 if tail is None else tail.shape[0])
    n_out = w.shape[1]
    tm = _largest_tile(lp, 528)
    n_groups = n_out // HEAD_DIM
    const = lambda b, i: (0, 0)
    in_specs = [pl.BlockSpec((1, tm, d), lambda b, i: (b, i, 0))]
    args = [h]
    if tail is not None:
        in_specs.append(pl.BlockSpec((tm, d), const))
        args.append(_place_tail(tail, tm, n_x))
    in_specs += [pl.BlockSpec((1, d), const), pl.BlockSpec((d, n_out), const)]
    args += [gain.reshape(1, d), w]
    return pl.pallas_call(
        functools.partial(_in_proj_kernel, n_x=n_x, has_tail=tail is not None),
        grid=(bsz, lp // tm),
        in_specs=in_specs,
        out_specs=pl.BlockSpec((n_groups, 1, tm, HEAD_DIM), lambda b, i: (0, b, i, 0)),
        out_shape=jax.ShapeDtypeStruct((n_groups, bsz, lp, HEAD_DIM), BF16),
        compiler_params=pltpu.CompilerParams(
            dimension_semantics=("parallel", "parallel"), vmem_limit_bytes=VMEM_LIMIT),
        name="in_proj",
    )(*args)


def _out_proj_kernel(*refs, n_x, has_tail, has_next):
    refs = list(refs)
    og_ref, w_ref, h_ref = refs[:3]
    tail_ref = refs.pop(3) if has_tail else None
    gain_ref = refs[3]
    next_refs = (refs.pop(4), refs.pop(4)) if has_next else None
    out_ref = refs[4]
    n_heads = og_ref.shape[0]
    lhs = jnp.concatenate([og_ref[h, 0] for h in range(n_heads)], axis=1)
    y = jnp.dot(lhs, w_ref[...], preferred_element_type=F32)
    ms = jnp.mean(y * y, axis=-1, keepdims=True)
    h_new = _tile_rows(h_ref, tail_ref, n_x) + y * lax.rsqrt(ms + EPS) * gain_ref[...]
    out_ref[0] = h_new
    if has_next:
        _project(h_new, next_refs[0], next_refs[1], refs[5])


def _out_proj(og, w, res, gain, tail=None, next_proj=None):
    n_heads, bsz, rows, _ = og.shape
    n_x, d = res.shape[1:]
    tm = _largest_tile(rows, 528 if next_proj is not None else 1024)
    const = lambda b, i: (0, 0)
    in_specs = [
        pl.BlockSpec((n_heads, 1, tm, HEAD_DIM), lambda b, i: (0, b, i, 0)),
        pl.BlockSpec((d, d), const),
        pl.BlockSpec((1, tm, d), lambda b, i: (b, i, 0)),
    ]
    args = [og, w, res]
    if tail is not None:
        assert rows == n_x + tail.shape[0]
        in_specs.append(pl.BlockSpec((tm, d), const))
        args.append(_place_tail(tail, tm, n_x))
    in_specs.append(pl.BlockSpec((1, d), const))
    args.append(gain.reshape(1, d))
    out_specs = [pl.BlockSpec((1, tm, d), lambda b, i: (b, i, 0))]
    out_shape = [jax.ShapeDtypeStruct((bsz, rows, d), F32)]
    if next_proj is not None:
        n_gain, n_w = next_proj
        n_groups = n_w.shape[1] // HEAD_DIM
        in_specs += [pl.BlockSpec((1, d), const), pl.BlockSpec(n_w.shape, const)]
        args += [n_gain.reshape(1, d), n_w]
        out_specs.append(pl.BlockSpec((n_groups, 1, tm, HEAD_DIM), lambda b, i: (0, b, i, 0)))
        out_shape.append(jax.ShapeDtypeStruct((n_groups, bsz, rows, HEAD_DIM), BF16))
    outs = pl.pallas_call(
        functools.partial(_out_proj_kernel, n_x=n_x, has_tail=tail is not None,
                          has_next=next_proj is not None),
        grid=(bsz, rows // tm),
        in_specs=in_specs,
        out_specs=out_specs,
        out_shape=out_shape,
        compiler_params=pltpu.CompilerParams(
            dimension_semantics=("parallel", "parallel"), vmem_limit_bytes=VMEM_LIMIT),
        name="out_proj",
    )(*args)
    return outs if next_proj is not None else outs[0]


N_LEVELS = 8


def _level_table():
    t = jnp.arange(BLOCK)[:, None]
    s = jnp.arange(BLOCK)[None, :]
    x = jnp.bitwise_xor(t, s)
    lvl = jnp.zeros((BLOCK, BLOCK), jnp.int32)
    for k in range(N_LEVELS - 1):
        lvl = jnp.where(x >= (1 << k), k + 1, lvl)
    return jnp.where(s > t, -1, lvl)


def _bcast_row(x, r):
    return jnp.broadcast_to(x[r:r + 1, :], x.shape)


def _hgrn_kernel(q_ref, f_ref, v_ref, g_ref, lb_ref, onorm_ref, lvl_ref, og_ref, st_ref, *,
                 n_heads, n_bat, lb_row):
    c = pl.program_id(1)

    @pl.when(c == 0)
    def _():
        st_ref[...] = jnp.zeros_like(st_ref)

    n_slab = BLOCK // SUBLANE
    row = lax.broadcasted_iota(jnp.int32, (SUBLANE, LANE), 0)
    not_tail = c != 0
    lvl = lvl_ref[...]
    onorm = onorm_ref[...]

    def head_body(bh):
        bi, h = divmod(bh, n_heads)
        q = q_ref[h, bi].astype(F32)
        fz = f_ref[h, bi].astype(F32)
        v = v_ref[h, bi]
        gate = g_ref[h, bi].astype(F32)

        lbl = lb_ref[h]
        e_lb = jnp.exp(lbl - jnp.max(lbl, axis=0, keepdims=True))
        lb = (jnp.sum(e_lb[:lb_row + 1], axis=0, keepdims=True)
              / jnp.sum(e_lb, axis=0, keepdims=True))

        e = jnp.exp(-jnp.abs(fz))
        inv = 1.0 / (1.0 + e)
        pos = fz >= 0
        sig = jnp.where(pos, inv, e * inv)
        sig_neg = jnp.where(pos, e * inv, inv)
        f_full = lb + (1.0 - lb) * sig
        k_full = (1.0 - lb) * sig_neg

        f_s, k_s, q_s = [], [], []
        for i in range(n_slab):
            sl = slice(i * SUBLANE, (i + 1) * SUBLANE)
            valid = jnp.logical_or(not_tail, row + i * SUBLANE < N_META)
            f_s.append(jnp.where(valid, f_full[sl], 1.0))
            k_s.append(jnp.where(valid, k_full[sl], 0.0))
            q_s.append(q[sl])

        def operands(p_s, r_s):
            qm = jnp.concatenate([a * b for a, b in zip(q_s, p_s)], axis=0).astype(BF16)
            km = jnp.concatenate([a * b for a, b in zip(k_s, r_s)], axis=0).astype(BF16)
            return qm, km

        def nt_dot(a, b):
            return lax.dot_general(a, b, (((1,), (1,)), ((), ())), preferred_element_type=F32)

        lvl_s = [lvl[i * SUBLANE:(i + 1) * SUBLANE] for i in range(n_slab)]

        def select_rows(a_s, level, res, slabs):
            for n, i in enumerate(slabs):
                a_s[i] = jnp.where(lvl_s[i] == level, res[n * SUBLANE:(n + 1) * SUBLANE], a_s[i])

        all_slabs = list(range(n_slab))
        qb = q_ref[h, bi]
        kb = jnp.concatenate(k_s, axis=0).astype(BF16)
        a_s = [jnp.zeros((SUBLANE, LANE), F32)] * n_slab
        select_rows(a_s, 0, nt_dot(qb, kb), all_slabs)
        p1 = f_s
        qm = jnp.concatenate([a * b for a, b in zip(q_s, p1)], axis=0).astype(BF16)
        select_rows(a_s, 1, nt_dot(qm, kb), all_slabs)
        odd = (row & 1) == 1
        p2 = [s * jnp.where(odd, pltpu.roll(s, 1, 0), 1.0) for s in p1]
        r2 = [jnp.where(odd, 1.0, pltpu.roll(s, SUBLANE - 1, 0)) for s in p1]
        select_rows(a_s, 2, nt_dot(*operands(p2, r2)), all_slabs)
        hi4 = (row & 3) >= 2
        lo8 = row < 4
        p4 = [s * jnp.where(hi4, jnp.where(lo8, _bcast_row(s, 1), _bcast_row(s, 5)), 1.0) for s in p2]
        r4 = [r * jnp.where(hi4, 1.0, jnp.where(lo8, _bcast_row(s, 3), _bcast_row(s, 7)))
              for s, r in zip(p2, r2)]
        select_rows(a_s, 3, nt_dot(*operands(p4, r4)), all_slabs)
        p_s = [s * jnp.where(lo8, 1.0, _bcast_row(s, 3)) for s in p4]
        r_s = [r * jnp.where(lo8, _bcast_row(s, 7), 1.0) for s, r in zip(p4, r4)]

        level = 4
        n_blk_slabs = 1
        while n_blk_slabs < n_slab:
            upper = [i for i in range(n_slab) if (i // n_blk_slabs) % 2 == 1]
            qm = jnp.concatenate([q_s[i] * p_s[i] for i in upper], axis=0).astype(BF16)
            km = jnp.concatenate([k_s[i] if i in upper else k_s[i] * r_s[i] for i in range(n_slab)],
                                 axis=0).astype(BF16)
            select_rows(a_s, level, nt_dot(qm, km), upper)
            n_blk = n_slab // n_blk_slabs
            tot = [_bcast_row(p_s[(b + 1) * n_blk_slabs - 1], SUBLANE - 1) for b in range(n_blk)]
            new_p, new_r = list(p_s), list(r_s)
            for b in range(n_blk):
                for i in range(b * n_blk_slabs, (b + 1) * n_blk_slabs):
                    if b % 2 == 1:
                        new_p[i] = p_s[i] * tot[b - 1]
                    else:
                        new_r[i] = r_s[i] * tot[b + 1]
            p_s, r_s = new_p, new_r
            n_blk_slabs *= 2
            level += 1
        a_mat = jnp.concatenate(a_s, axis=0)

        qm, km = operands(p_s, r_s)
        st = st_ref[bi, h]
        o = jnp.dot(a_mat.astype(BF16), v, preferred_element_type=F32) + nt_dot(qm, st.astype(BF16))
        d_last = p_s[n_slab - 1][SUBLANE - 1:SUBLANE, :]
        kv_t = lax.dot_general(v, km, (((0,), (0,)), ((), ())), preferred_element_type=F32)
        st_ref[bi, h] = st * d_last + kv_t

        ms = jnp.mean(o * o, axis=-1, keepdims=True)
        o = o * lax.rsqrt(ms + EPS) * onorm
        og_ref[h, bi] = (o * (gate * jax.nn.sigmoid(gate))).astype(BF16)

    for bh in range(n_bat * n_heads):
        head_body(bh)


def _hgrn_mixer(proj, lb, out_norm, layer_idx):
    n_groups, bsz, lp, _ = proj.shape
    n_heads = n_groups // 4
    n_chunks = lp // BLOCK
    n_rows = lb.shape[0]
    n_bat = HGRN_BATCH if bsz % HGRN_BATCH == 0 else 1
    lb_h = lb.reshape(n_rows, n_heads, HEAD_DIM).transpose(1, 0, 2)

    def blk(group):
        return pl.BlockSpec((n_heads, n_bat, BLOCK, HEAD_DIM),
                            lambda b, c: (group, b, (c + n_chunks - 1) % n_chunks, 0))

    return pl.pallas_call(
        functools.partial(_hgrn_kernel, n_heads=n_heads, n_bat=n_bat, lb_row=layer_idx),
        grid=(bsz // n_bat, n_chunks),
        in_specs=[
            blk(0), blk(1), blk(2), blk(3),
            pl.BlockSpec((n_heads, n_rows, HEAD_DIM), lambda b, c: (0, 0, 0)),
            pl.BlockSpec((1, HEAD_DIM), lambda b, c: (0, 0)),
            pl.BlockSpec((BLOCK, BLOCK), lambda b, c: (0, 0)),
        ],
        out_specs=blk(0),
        out_shape=jax.ShapeDtypeStruct((n_heads, bsz, lp, HEAD_DIM), BF16),
        scratch_shapes=[pltpu.VMEM((n_bat, n_heads, HEAD_DIM, HEAD_DIM), F32)],
        compiler_params=pltpu.CompilerParams(
            dimension_semantics=("parallel", "arbitrary"), vmem_limit_bytes=VMEM_LIMIT),
        name="hgrn_mixer",
    )(proj, proj, proj, proj, lb_h, out_norm.reshape(1, HEAD_DIM), _level_table())


def _sb_kernel(q_ref, k_ref, v_ref, g_ref, tri_ref, og_ref, *, tk, n_x, scale):
    q = (q_ref[0, 0].astype(F32) * (scale * LOG2E)).astype(BF16)
    tri = tri_ref[...]

    def masked(x, mask):
        rows = mask.shape[0]
        head = jnp.where(mask, x[:rows], 0.0)
        return head if rows == x.shape[0] else jnp.concatenate([head, x[rows:]], axis=0)

    def tile(qt, start, width, tri_t, mask, carry, acc):
        kt = k_ref[0, 0, start:start + width, :]
        vt = v_ref[0, 0, start:start + width, :]
        z = lax.dot_general(qt, kt, (((1,), (1,)), ((), ())), preferred_element_type=F32)
        sp = masked(jnp.maximum(z, 0.0) + jnp.log2(1.0 + jnp.exp2(-jnp.abs(z))), mask)
        cw = jnp.dot(sp.astype(BF16), tri_t, preferred_element_type=F32)
        a = masked(jnp.exp2(z - cw - carry), mask)
        return carry + cw[:, :1], acc + jnp.dot(a.astype(BF16), vt, preferred_element_type=F32)

    strict = (lax.broadcasted_iota(jnp.int32, (tk, tk), 1)
              < lax.broadcasted_iota(jnp.int32, (tk, tk), 0))
    carry = jnp.zeros((n_x, 1), F32)
    acc = jnp.zeros((n_x, HEAD_DIM), F32)
    for j in range(n_x // tk - 1, -1, -1):
        r0 = j * tk
        c_j, acc_j = tile(q[r0:], r0, tk, tri, strict, carry[r0:], acc[r0:])
        if r0:
            c_j = jnp.concatenate([carry[:r0], c_j], axis=0)
            acc_j = jnp.concatenate([acc[:r0], acc_j], axis=0)
        carry, acc = c_j, acc_j

    meta_ok = lax.broadcasted_iota(jnp.int32, (n_x, BLOCK), 1) < N_META
    _, acc_tail = tile(q, n_x, BLOCK, tri[:BLOCK, :BLOCK], meta_ok, jnp.zeros((n_x, 1), F32),
                       jnp.zeros((n_x, HEAD_DIM), F32))

    acc = acc + jnp.exp2(-carry) * acc_tail
    gate = g_ref[0, 0].astype(F32)
    og_ref[0, 0] = (acc * (gate * jax.nn.sigmoid(gate))).astype(BF16)


def _sb_mixer(proj, n_x):
    n_groups, bsz, lp, _ = proj.shape
    n_heads = n_groups // 4
    tk = _largest_tile(n_x, 256, BLOCK)
    tri = (jnp.arange(tk)[:, None] >= jnp.arange(tk)[None, :]).astype(BF16)

    def rows(group):
        return pl.BlockSpec((1, 1, n_x, HEAD_DIM), lambda b, h: (group * n_heads + h, b, 0, 0))

    def full(group):
        return pl.BlockSpec((1, 1, lp, HEAD_DIM), lambda b, h: (group * n_heads + h, b, 0, 0))

    return pl.pallas_call(
        functools.partial(_sb_kernel, tk=tk, n_x=n_x, scale=HEAD_DIM ** -0.5),
        grid=(bsz, n_heads),
        in_specs=[rows(0), full(1), full(2), rows(3), pl.BlockSpec((tk, tk), lambda b, h: (0, 0))],
        out_specs=pl.BlockSpec((1, 1, n_x, HEAD_DIM), lambda b, h: (h, b, 0, 0)),
        out_shape=jax.ShapeDtypeStruct((n_heads, bsz, n_x, HEAD_DIM), BF16),
        compiler_params=pltpu.CompilerParams(
            dimension_semantics=("parallel", "parallel"), vmem_limit_bytes=VMEM_LIMIT),
        name="sb_mixer",
    )(proj, proj, proj, proj, tri)


def kernel(x, meta_tokens, pre_norm, post_norm, hgrn_w_in, hgrn_lb, hgrn_out_norm, hgrn_w_out,
           sb_w_in, sb_w_out):
    bsz, n_x, d = x.shape
    depth = pre_norm.shape[0]
    assert depth == 2 and meta_tokens.shape[0] == N_META and n_x % BLOCK == 0 and d % HEAD_DIM == 0

    tail = jnp.concatenate([meta_tokens.astype(x.dtype), jnp.zeros((BLOCK - N_META, d), x.dtype)], axis=0)

    proj = _in_proj(x, pre_norm[0], hgrn_w_in[0].astype(BF16), tail)
    og = _hgrn_mixer(proj, hgrn_lb.astype(F32), hgrn_out_norm[0], 0)
    h, proj = _out_proj(og, hgrn_w_out[0].astype(BF16), x, post_norm[0], tail,
                        next_proj=(pre_norm[1], sb_w_in[0].astype(BF16)))

    og = _sb_mixer(proj, n_x)
    return _out_proj(og, sb_w_out[0].astype(BF16), h, post_norm[1])
```

```python
import functools
import math

import jax
import jax.numpy as jnp
from jax import lax
from jax.experimental import pallas as pl
from jax.experimental.pallas import tpu as pltpu

LANE = 128
SUBLANE = 8
HEAD_DIM = 128
BLOCK = 128
N_META = 16
EPS = 1e-6
LOG2E = 1.4426950408889634
VMEM_LIMIT = 48 * 1024 * 1024
MXU_DIM = 256
PROJ_ROW_TILE = 528
OUT_ROW_TILE = 1024
HGRN_BATCH = 8
PROJ_CHUNK = 512
F32 = jnp.float32
BF16 = jnp.bfloat16


def _largest_tile(n, cap, align=16):
    best = None
    for t in range(align, min(n, cap) + 1, align):
        if n % t == 0:
            best = t
    assert best is not None, (n, cap)
    return best


def _tile_rows(x_ref, tail_ref, n_x):
    rows = x_ref[0]
    if tail_ref is None:
        return rows
    tm = rows.shape[0]
    row = pl.program_id(1) * tm + lax.broadcasted_iota(jnp.int32, (tm, 1), 0)
    return jnp.where(row >= n_x, tail_ref[...], rows)


def _place_tail(tail, tm, n_x):
    off = n_x % tm
    assert off + tail.shape[0] == tm
    return jnp.concatenate([jnp.zeros((off, tail.shape[1]), tail.dtype), tail], axis=0)


def _project(x, gain_ref, w_ref, out_ref):
    ms = jnp.mean(x * x, axis=-1, keepdims=True)
    xn = (x * lax.rsqrt(ms + EPS) * gain_ref[...]).astype(BF16)
    n_out = w_ref.shape[1]
    for c in range(0, n_out, PROJ_CHUNK):
        res = jnp.dot(xn, w_ref[:, c:c + PROJ_CHUNK], preferred_element_type=F32)
        for j in range(PROJ_CHUNK // HEAD_DIM):
            out_ref[c // HEAD_DIM + j, 0] = res[:, j * HEAD_DIM:(j + 1) * HEAD_DIM].astype(BF16)


def _in_proj_kernel(*refs, n_x, has_tail):
    h_ref, tail_ref, gain_ref, w_ref, out_ref = refs if has_tail else (refs[0], None) + refs[1:]
    _project(_tile_rows(h_ref, tail_ref, n_x), gain_ref, w_ref, out_ref)


def _in_proj(h, gain, w, tail=None):
    bsz, n_x, d = h.shape
    lp = n_x + (0 if tail is None else tail.shape[0])
    n_out = w.shape[1]
    tm = _largest_tile(lp, PROJ_ROW_TILE)
    n_groups = n_out // HEAD_DIM
    const = lambda b, i: (0, 0)
    in_specs = [pl.BlockSpec((1, tm, d), lambda b, i: (b, i, 0))]
    args = [h]
    if tail is not None:
        in_specs.append(pl.BlockSpec((tm, d), const))
        args.append(_place_tail(tail, tm, n_x))
    in_specs += [pl.BlockSpec((1, d), const), pl.BlockSpec((d, n_out), const)]
    args += [gain.reshape(1, d), w]
    return pl.pallas_call(
        functools.partial(_in_proj_kernel, n_x=n_x, has_tail=tail is not None),
        grid=(bsz, lp // tm),
        in_specs=in_specs,
        out_specs=pl.BlockSpec((n_groups, 1, tm, HEAD_DIM), lambda b, i: (0, b, i, 0)),
        out_shape=jax.ShapeDtypeStruct((n_groups, bsz, lp, HEAD_DIM), BF16),
        compiler_params=pltpu.CompilerParams(
            dimension_semantics=("parallel", "parallel"), vmem_limit_bytes=VMEM_LIMIT),
        name="in_proj",
    )(*args)


def _out_proj_kernel(*refs, n_x, has_tail, has_next):
    refs = list(refs)
    og_ref, w_ref, h_ref = refs[:3]
    tail_ref = refs.pop(3) if has_tail else None
    gain_ref = refs[3]
    next_refs = (refs.pop(4), refs.pop(4)) if has_next else None
    out_ref = refs[4]
    n_heads = og_ref.shape[0]
    lhs = jnp.concatenate([og_ref[h, 0] for h in range(n_heads)], axis=1)
    y = jnp.dot(lhs, w_ref[...], preferred_element_type=F32)
    ms = jnp.mean(y * y, axis=-1, keepdims=True)
    h_new = _tile_rows(h_ref, tail_ref, n_x) + y * lax.rsqrt(ms + EPS) * gain_ref[...]
    out_ref[0] = h_new
    if has_next:
        _project(h_new, next_refs[0], next_refs[1], refs[5])


def _out_proj(og, w, res, gain, tail=None, next_proj=None):
    n_heads, bsz, rows, _ = og.shape
    n_x, d = res.shape[1:]
    tm = _largest_tile(rows, PROJ_ROW_TILE if next_proj is not None else OUT_ROW_TILE)
    const = lambda b, i: (0, 0)
    in_specs = [
        pl.BlockSpec((n_heads, 1, tm, HEAD_DIM), lambda b, i: (0, b, i, 0)),
        pl.BlockSpec((d, d), const),
        pl.BlockSpec((1, tm, d), lambda b, i: (b, i, 0)),
    ]
    args = [og, w, res]
    if tail is not None:
        assert rows == n_x + tail.shape[0]
        in_specs.append(pl.BlockSpec((tm, d), const))
        args.append(_place_tail(tail, tm, n_x))
    in_specs.append(pl.BlockSpec((1, d), const))
    args.append(gain.reshape(1, d))
    out_specs = [pl.BlockSpec((1, tm, d), lambda b, i: (b, i, 0))]
    out_shape = [jax.ShapeDtypeStruct((bsz, rows, d), F32)]
    if next_proj is not None:
        n_gain, n_w = next_proj
        n_groups = n_w.shape[1] // HEAD_DIM
        in_specs += [pl.BlockSpec((1, d), const), pl.BlockSpec(n_w.shape, const)]
        args += [n_gain.reshape(1, d), n_w]
        out_specs.append(pl.BlockSpec((n_groups, 1, tm, HEAD_DIM), lambda b, i: (0, b, i, 0)))
        out_shape.append(jax.ShapeDtypeStruct((n_groups, bsz, rows, HEAD_DIM), BF16))
    outs = pl.pallas_call(
        functools.partial(_out_proj_kernel, n_x=n_x, has_tail=tail is not None,
                          has_next=next_proj is not None),
        grid=(bsz, rows // tm),
        in_specs=in_specs,
        out_specs=out_specs,
        out_shape=out_shape,
        compiler_params=pltpu.CompilerParams(
            dimension_semantics=("parallel", "parallel"), vmem_limit_bytes=VMEM_LIMIT),
        name="out_proj",
    )(*args)
    return outs if next_proj is not None else outs[0]


N_LEVELS = 8


def _level_table():
    t = jnp.arange(BLOCK)[:, None]
    s = jnp.arange(BLOCK)[None, :]
    x = jnp.bitwise_xor(t, s)
    lvl = jnp.zeros((BLOCK, BLOCK), jnp.int32)
    for k in range(N_LEVELS - 1):
        lvl = jnp.where(x >= (1 << k), k + 1, lvl)
    return jnp.where(s > t, -1, lvl)


def _bcast_row(x, r):
    return jnp.broadcast_to(x[r:r + 1, :], x.shape)


def _hgrn_kernel(q_ref, f_ref, v_ref, g_ref, lb_ref, onorm_ref, lvl_ref, og_ref, st_ref, *,
                 n_heads, n_bat, lb_row):
    c = pl.program_id(1)

    @pl.when(c == 0)
    def _():
        st_ref[...] = jnp.zeros_like(st_ref)

    n_slab = BLOCK // SUBLANE
    row = lax.broadcasted_iota(jnp.int32, (SUBLANE, LANE), 0)
    not_tail = c != 0
    lvl = lvl_ref[...]
    onorm = onorm_ref[...]

    def head_body(bh):
        bi, h = divmod(bh, n_heads)
        q = q_ref[h, bi].astype(F32)
        fz = f_ref[h, bi].astype(F32)
        v = v_ref[h, bi]
        gate = g_ref[h, bi].astype(F32)

        lbl = lb_ref[h]
        e_lb = jnp.exp(lbl - jnp.max(lbl, axis=0, keepdims=True))
        lb = (jnp.sum(e_lb[:lb_row + 1], axis=0, keepdims=True)
              / jnp.sum(e_lb, axis=0, keepdims=True))

        e = jnp.exp(-jnp.abs(fz))
        inv = 1.0 / (1.0 + e)
        pos = fz >= 0
        sig = jnp.where(pos, inv, e * inv)
        sig_neg = jnp.where(pos, e * inv, inv)
        f_full = lb + (1.0 - lb) * sig
        k_full = (1.0 - lb) * sig_neg

        f_s, k_s, q_s = [], [], []
        for i in range(n_slab):
            sl = slice(i * SUBLANE, (i + 1) * SUBLANE)
            valid = jnp.logical_or(not_tail, row + i * SUBLANE < N_META)
            f_s.append(jnp.where(valid, f_full[sl], 1.0))
            k_s.append(jnp.where(valid, k_full[sl], 0.0))
            q_s.append(q[sl])

        def operands(p_s, r_s):
            qm = jnp.concatenate([a * b for a, b in zip(q_s, p_s)], axis=0).astype(BF16)
            km = jnp.concatenate([a * b for a, b in zip(k_s, r_s)], axis=0).astype(BF16)
            return qm, km

        def nt_dot(a, b):
            return lax.dot_general(a, b, (((1,), (1,)), ((), ())), preferred_element_type=F32)

        lvl_s = [lvl[i * SUBLANE:(i + 1) * SUBLANE] for i in range(n_slab)]

        def select_rows(a_s, level, res, slabs):
            for n, i in enumerate(slabs):
                a_s[i] = jnp.where(lvl_s[i] == level, res[n * SUBLANE:(n + 1) * SUBLANE], a_s[i])

        all_slabs = list(range(n_slab))
        qb = q_ref[h, bi]
        kb = jnp.concatenate(k_s, axis=0).astype(BF16)
        a_s = [jnp.zeros((SUBLANE, LANE), F32)] * n_slab
        select_rows(a_s, 0, nt_dot(qb, kb), all_slabs)
        p1 = f_s
        qm = jnp.concatenate([a * b for a, b in zip(q_s, p1)], axis=0).astype(BF16)
        select_rows(a_s, 1, nt_dot(qm, kb), all_slabs)
        odd = (row & 1) == 1
        p2 = [s * jnp.where(odd, pltpu.roll(s, 1, 0), 1.0) for s in p1]
        r2 = [jnp.where(odd, 1.0, pltpu.roll(s, SUBLANE - 1, 0)) for s in p1]
        select_rows(a_s, 2, nt_dot(*operands(p2, r2)), all_slabs)
        hi4 = (row & 3) >= 2
        lo8 = row < 4
        p4 = [s * jnp.where(hi4, jnp.where(lo8, _bcast_row(s, 1), _bcast_row(s, 5)), 1.0) for s in p2]
        r4 = [r * jnp.where(hi4, 1.0, jnp.where(lo8, _bcast_row(s, 3), _bcast_row(s, 7)))
              for s, r in zip(p2, r2)]
        select_rows(a_s, 3, nt_dot(*operands(p4, r4)), all_slabs)
        p_s = [s * jnp.where(lo8, 1.0, _bcast_row(s, 3)) for s in p4]
        r_s = [r * jnp.where(lo8, _bcast_row(s, 7), 1.0) for s, r in zip(p4, r4)]

        level = 4
        n_blk_slabs = 1
        while n_blk_slabs < n_slab:
            upper = [i for i in range(n_slab) if (i // n_blk_slabs) % 2 == 1]
            qm = jnp.concatenate([q_s[i] * p_s[i] for i in upper], axis=0).astype(BF16)
            km = jnp.concatenate([k_s[i] if i in upper else k_s[i] * r_s[i] for i in range(n_slab)],
                                 axis=0).astype(BF16)
            select_rows(a_s, level, nt_dot(qm, km), upper)
            n_blk = n_slab // n_blk_slabs
            tot = [_bcast_row(p_s[(b + 1) * n_blk_slabs - 1], SUBLANE - 1) for b in range(n_blk)]
            new_p, new_r = list(p_s), list(r_s)
            for b in range(n_blk):
                for i in range(b * n_blk_slabs, (b + 1) * n_blk_slabs):
                    if b % 2 == 1:
                        new_p[i] = p_s[i] * tot[b - 1]
                    else:
                        new_r[i] = r_s[i] * tot[b + 1]
            p_s, r_s = new_p, new_r
            n_blk_slabs *= 2
            level += 1
        a_mat = jnp.concatenate(a_s, axis=0)

        qm, km = operands(p_s, r_s)
        st = st_ref[bi, h]
        o = jnp.dot(a_mat.astype(BF16), v, preferred_element_type=F32) + nt_dot(qm, st.astype(BF16))
        d_last = p_s[n_slab - 1][SUBLANE - 1:SUBLANE, :]
        kv_t = lax.dot_general(v, km, (((0,), (0,)), ((), ())), preferred_element_type=F32)
        st_ref[bi, h] = st * d_last + kv_t

        ms = jnp.mean(o * o, axis=-1, keepdims=True)
        o = o * lax.rsqrt(ms + EPS) * onorm
        og_ref[h, bi] = (o * (gate * jax.nn.sigmoid(gate))).astype(BF16)

    for bh in range(n_bat * n_heads):
        head_body(bh)


def _hgrn_mixer(proj, lb, out_norm, layer_idx):
    n_groups, bsz, lp, _ = proj.shape
    n_heads = n_groups // 4
    n_chunks = lp // BLOCK
    n_rows = lb.shape[0]
    n_bat = HGRN_BATCH if bsz % HGRN_BATCH == 0 else 1
    lb_h = lb.reshape(n_rows, n_heads, HEAD_DIM).transpose(1, 0, 2)

    def blk(group):
        return pl.BlockSpec((n_heads, n_bat, BLOCK, HEAD_DIM),
                            lambda b, c: (group, b, (c + n_chunks - 1) % n_chunks, 0))

    return pl.pallas_call(
        functools.partial(_hgrn_kernel, n_heads=n_heads, n_bat=n_bat, lb_row=layer_idx),
        grid=(bsz // n_bat, n_chunks),
        in_specs=[
            blk(0), blk(1), blk(2), blk(3),
            pl.BlockSpec((n_heads, n_rows, HEAD_DIM), lambda b, c: (0, 0, 0)),
            pl.BlockSpec((1, HEAD_DIM), lambda b, c: (0, 0)),
            pl.BlockSpec((BLOCK, BLOCK), lambda b, c: (0, 0)),
        ],
        out_specs=blk(0),
        out_shape=jax.ShapeDtypeStruct((n_heads, bsz, lp, HEAD_DIM), BF16),
        scratch_shapes=[pltpu.VMEM((n_bat, n_heads, HEAD_DIM, HEAD_DIM), F32)],
        compiler_params=pltpu.CompilerParams(
            dimension_semantics=("parallel", "arbitrary"), vmem_limit_bytes=VMEM_LIMIT),
        name="hgrn_mixer",
    )(proj, proj, proj, proj, lb_h, out_norm.reshape(1, HEAD_DIM), _level_table())


def _sb_kernel(q_ref, k_ref, v_ref, g_ref, tri_ref, og_ref, *, tk, n_x, scale):
    q = (q_ref[0, 0].astype(F32) * (scale * LOG2E)).astype(BF16)
    tri = tri_ref[...]

    def masked(x, mask):
        rows = mask.shape[0]
        head = jnp.where(mask, x[:rows], 0.0)
        return head if rows == x.shape[0] else jnp.concatenate([head, x[rows:]], axis=0)

    def tile(qt, start, width, tri_t, mask, carry, acc):
        kt = k_ref[0, 0, start:start + width, :]
        vt = v_ref[0, 0, start:start + width, :]
        z = lax.dot_general(qt, kt, (((1,), (1,)), ((), ())), preferred_element_type=F32)
        sp = masked(jnp.maximum(z, 0.0) + jnp.log2(1.0 + jnp.exp2(-jnp.abs(z))), mask)
        cw = jnp.dot(sp.astype(BF16), tri_t, preferred_element_type=F32)
        a = masked(jnp.exp2(z - cw - carry), mask)
        return carry + cw[:, :1], acc + jnp.dot(a.astype(BF16), vt, preferred_element_type=F32)

    strict = (lax.broadcasted_iota(jnp.int32, (tk, tk), 1)
              < lax.broadcasted_iota(jnp.int32, (tk, tk), 0))
    carry = jnp.zeros((n_x, 1), F32)
    acc = jnp.zeros((n_x, HEAD_DIM), F32)
    for j in range(n_x // tk - 1, -1, -1):
        r0 = j * tk
        c_j, acc_j = tile(q[r0:], r0, tk, tri, strict, carry[r0:], acc[r0:])
        if r0:
            c_j = jnp.concatenate([carry[:r0], c_j], axis=0)
            acc_j = jnp.concatenate([acc[:r0], acc_j], axis=0)
        carry, acc = c_j, acc_j

    meta_ok = lax.broadcasted_iota(jnp.int32, (n_x, BLOCK), 1) < N_META
    _, acc_tail = tile(q, n_x, BLOCK, tri[:BLOCK, :BLOCK], meta_ok, jnp.zeros((n_x, 1), F32),
                       jnp.zeros((n_x, HEAD_DIM), F32))

    acc = acc + jnp.exp2(-carry) * acc_tail
    gate = g_ref[0, 0].astype(F32)
    og_ref[0, 0] = (acc * (gate * jax.nn.sigmoid(gate))).astype(BF16)


def _sb_mixer(proj, n_x):
    n_groups, bsz, lp, _ = proj.shape
    n_heads = n_groups // 4
    tk = _largest_tile(n_x, MXU_DIM, BLOCK)
    tri = (jnp.arange(tk)[:, None] >= jnp.arange(tk)[None, :]).astype(BF16)

    def rows(group):
        return pl.BlockSpec((1, 1, n_x, HEAD_DIM), lambda b, h: (group * n_heads + h, b, 0, 0))

    def full(group):
        return pl.BlockSpec((1, 1, lp, HEAD_DIM), lambda b, h: (group * n_heads + h, b, 0, 0))

    return pl.pallas_call(
        functools.partial(_sb_kernel, tk=tk, n_x=n_x, scale=HEAD_DIM ** -0.5),
        grid=(bsz, n_heads),
        in_specs=[rows(0), full(1), full(2), rows(3), pl.BlockSpec((tk, tk), lambda b, h: (0, 0))],
        out_specs=pl.BlockSpec((1, 1, n_x, HEAD_DIM), lambda b, h: (h, b, 0, 0)),
        out_shape=jax.ShapeDtypeStruct((n_heads, bsz, n_x, HEAD_DIM), BF16),
        compiler_params=pltpu.CompilerParams(
            dimension_semantics=("parallel", "parallel"), vmem_limit_bytes=VMEM_LIMIT),
        name="sb_mixer",
    )(proj, proj, proj, proj, tri)


def kernel(x, meta_tokens, pre_norm, post_norm, hgrn_w_in, hgrn_lb, hgrn_out_norm, hgrn_w_out,
           sb_w_in, sb_w_out):
    bsz, n_x, d = x.shape
    depth = pre_norm.shape[0]
    assert depth == 2 and meta_tokens.shape[0] == N_META and n_x % BLOCK == 0 and d % HEAD_DIM == 0

    tail = jnp.concatenate([meta_tokens.astype(x.dtype), jnp.zeros((BLOCK - N_META, d), x.dtype)], axis=0)

    proj = _in_proj(x, pre_norm[0], hgrn_w_in[0].astype(BF16), tail)
    og = _hgrn_mixer(proj, hgrn_lb.astype(F32), hgrn_out_norm[0], 0)
    h, proj = _out_proj(og, hgrn_w_out[0].astype(BF16), x, post_norm[0], tail,
                        next_proj=(pre_norm[1], sb_w_in[0].astype(BF16)))

    og = _sb_mixer(proj, n_x)
    return _out_proj(og, sb_w_out[0].astype(BF16), h, post_norm[1])
```
